```python
import jax, jax.numpy as jnp
from jax import lax
import numpy as np

D_MODEL = 1024
BATCH = 8
SEQ = 2048
DEPTH = 2
DEC_BATCH = 128
DEC_SEQ = 1
PAST_LEN = 2048
PAGE_SIZE = 128

HEAD_DIM = 64
ROT_DIM = HEAD_DIM // 4
ROPE_THETA = 500000.0
Q_BLOCK = 128
EPS = 1e-6
FOX_HEADS = 8
FOX_WIDTH = FOX_HEADS * HEAD_DIM
RNN_WIDTH = D_MODEL // 2
RNN_BLOCKS = 8
RNN_BLOCK_W = RNN_WIDTH // RNN_BLOCKS
CONV_W = 4
LRU_C = 8.0
DSA_HEADS = D_MODEL // HEAD_DIM
DSA_KV_HEADS = 4
DSA_GROUP = DSA_HEADS // DSA_KV_HEADS
IDX_HEADS = 8
IDX_DIM = 64
TOPK_MAX = 256
D_FF = 4 * D_MODEL
N_FOX_LAYERS = (DEPTH + 1) // 2
N_DSA_LAYERS = DEPTH // 2
AB_SIZES = (FOX_WIDTH, FOX_WIDTH, FOX_WIDTH, FOX_HEADS, RNN_WIDTH, RNN_WIDTH)
C_SIZES = (DSA_HEADS * HEAD_DIM, DSA_KV_HEADS * HEAD_DIM, DSA_KV_HEADS * HEAD_DIM, IDX_HEADS * IDX_DIM, IDX_DIM, IDX_HEADS)

kernel_name = 'hybrid_fox_rglru_dsa_step'


def split_cols(x, sizes):
    return jnp.split(x, [int(c) for c in np.cumsum(sizes)[:-1]], axis=-1)


def rms_norm(x, g):
    xf = x.astype(jnp.float32)
    y = xf * lax.rsqrt(jnp.mean(xf * xf, axis=-1, keepdims=True) + EPS)
    return (y * g.astype(jnp.float32)).astype(x.dtype)


def layer_norm(x, g, b):
    xf = x.astype(jnp.float32)
    mu = jnp.mean(xf, axis=-1, keepdims=True)
    var = jnp.mean(jnp.square(xf - mu), axis=-1, keepdims=True)
    return ((xf - mu) * lax.rsqrt(var + EPS) * g.astype(jnp.float32) + b.astype(jnp.float32)).astype(x.dtype)


def partial_rope(x, pos):
    half = ROT_DIM // 2
    inv = ROPE_THETA ** (-jnp.arange(half, dtype=jnp.float32) / half)
    ang = pos.astype(jnp.float32)[:, None] * inv[None, :]
    cos = jnp.cos(ang)[None, :, None, :]
    sin = jnp.sin(ang)[None, :, None, :]
    xr = x[..., :ROT_DIM].astype(jnp.float32)
    x1, x2 = xr[..., :half], xr[..., half:]
    rot = jnp.concatenate([x1 * cos - x2 * sin, x2 * cos + x1 * sin], axis=-1)
    return jnp.concatenate([rot.astype(x.dtype), x[..., ROT_DIM:]], axis=-1)


def gather_pages(pool, page_table):
    g = pool[page_table]
    return g.reshape((g.shape[0], g.shape[1] * g.shape[2]) + g.shape[3:])


def map_query_blocks(fn, qpos, *qs):
    nb = qpos.shape[0] // Q_BLOCK
    blocks = tuple(jnp.swapaxes(a.reshape((a.shape[0], nb, Q_BLOCK) + a.shape[2:]), 0, 1) for a in qs)
    out = lax.map(lambda args: fn(*args), (qpos.reshape(nb, Q_BLOCK),) + blocks)
    out = jnp.swapaxes(out, 0, 1)
    return out.reshape((out.shape[0], nb * Q_BLOCK) + out.shape[3:])


def fox_attend(qpos, q, cq, k, v, ck, kpos):
    s = jnp.einsum('bqhd,bkhd->bhqk', q, k).astype(jnp.float32) * (HEAD_DIM ** -0.5)
    s = s + (jnp.swapaxes(cq, 1, 2)[..., :, None] - jnp.swapaxes(ck, 1, 2)[..., None, :])
    s = jnp.where(kpos[None, None, None, :] <= qpos[None, None, :, None], s, -jnp.inf)
    p = jax.nn.softmax(s, axis=-1).astype(v.dtype)
    return jnp.einsum('bhqk,bkhd->bqhd', p, v)


def rglru_branch(xr, conv_prev, h0, conv_w, conv_b, ga_w, ga_b, gx_w, gx_b, lam):
    B, T, _ = xr.shape
    xp = jnp.concatenate([conv_prev.astype(xr.dtype), xr], axis=1)
    xc = conv_b + sum(xp[:, j:j + T] * conv_w[j] for j in range(CONV_W))
    xb = xc.reshape(B, T, RNN_BLOCKS, RNN_BLOCK_W)
    r = jax.nn.sigmoid(jnp.einsum('btnd,nde->btne', xb, ga_w).reshape(B, T, RNN_WIDTH) + ga_b)
    i = jax.nn.sigmoid(jnp.einsum('btnd,nde->btne', xb, gx_w).reshape(B, T, RNN_WIDTH) + gx_b)
    log_a = -LRU_C * r.astype(jnp.float32) * jax.nn.softplus(-lam.astype(jnp.float32))
    a = jnp.exp(log_a)
    u = jnp.sqrt(-jnp.expm1(2.0 * log_a)) * (i * xc).astype(jnp.float32)

    def step(h, au):
        a_t, u_t = au
        h = a_t * h + u_t
        return h, h

    hT, hs = lax.scan(step, h0.astype(jnp.float32), (jnp.swapaxes(a, 0, 1), jnp.swapaxes(u, 0, 1)))
    return jnp.swapaxes(hs, 0, 1).astype(xr.dtype), hT, xp[:, T:]


def fox_lru_mixer(xn, pos, conv_prev, h0, past_kv, past_logf, page_table,
                  w_in, b_f, conv_w, conv_b, ga_w, ga_b, gx_w, gx_b, lam, w_out):
    B, T, _ = xn.shape
    q, k, v, f, xr, gate = split_cols(xn @ w_in, AB_SIZES)
    shp = (B, T, FOX_HEADS, HEAD_DIM)
    q, k, v = q.reshape(shp), k.reshape(shp), v.reshape(shp)
    logf = jax.nn.log_sigmoid((f + b_f).astype(jnp.float32))
    if page_table is None:
        c = jnp.cumsum(logf, axis=1)
        attn = map_query_blocks(lambda pb, qb, cb: fox_attend(pb, qb, cb, k, v, c, pos), pos, q, c)
    else:
        past = gather_pages(past_kv, page_table)
        k_all = jnp.concatenate([past[:, :, 0], k], axis=1)
        v_all = jnp.concatenate([past[:, :, 1], v], axis=1)
        logf_all = jnp.concatenate([gather_pages(past_logf, page_table).astype(jnp.float32), logf], axis=1)
        c = jnp.cumsum(logf_all, axis=1)
        attn = fox_attend(pos, q, c[:, -T:], k_all, v_all, c, jnp.arange(c.shape[1]))
    y, hT, conv_new = rglru_branch(xr, conv_prev, h0, conv_w, conv_b, ga_w, ga_b, gx_w, gx_b, lam)
    mixed = jnp.concatenate([attn.reshape(B, T, FOX_WIDTH), y * jax.nn.gelu(gate)], axis=-1)
    state = (jnp.stack([k, v], axis=2), logf.astype(xn.dtype), conv_new, hT.astype(xn.dtype))
    return mixed @ w_out, state


def dsa_attend(qpos, q, qi, wi, k, v, ki, topk):
    B, Tq = q.shape[:2]
    kpos = jnp.arange(k.shape[1])
    dots = jnp.einsum('bqhd,bkd->bqhk', qi, ki).astype(jnp.float32)
    score = jnp.einsum('bqhk,bqh->bqk', jax.nn.relu(dots), wi.astype(jnp.float32))
    causal = kpos[None, None, :] <= qpos[None, :, None]
    _, idx = lax.top_k(jnp.where(causal, score, -jnp.inf), topk)
    gather = jax.vmap(lambda rows, ids: rows[ids])
    k_sel = gather(k, idx)
    v_sel = gather(v, idx)
    qg = q.reshape(B, Tq, DSA_KV_HEADS, DSA_GROUP, HEAD_DIM)
    s = jnp.einsum('bqngd,bqknd->bqngk', qg, k_sel).astype(jnp.float32) * (HEAD_DIM ** -0.5)
    valid = (idx <= qpos[None, :, None])[:, :, None, None, :]
    p = jax.nn.softmax(jnp.where(valid, s, -jnp.inf), axis=-1).astype(v.dtype)
    o = jnp.einsum('bqngk,bqknd->bqngd', p, v_sel)
    return o.reshape(B, Tq, DSA_HEADS * HEAD_DIM)


def dsa_mixer(xn, pos, past_kv, past_idx_k, page_table, w_in, idx_g, idx_b, w_out):
    B, T, _ = xn.shape
    q, k, v, qi, ki, wi = split_cols(xn @ w_in, C_SIZES)
    q = partial_rope(q.reshape(B, T, DSA_HEADS, HEAD_DIM), pos)
    k = partial_rope(k.reshape(B, T, DSA_KV_HEADS, HEAD_DIM), pos)
    v = v.reshape(B, T, DSA_KV_HEADS, HEAD_DIM)
    qi = partial_rope(qi.reshape(B, T, IDX_HEADS, IDX_DIM), pos)
    ki = partial_rope(layer_norm(ki, idx_g, idx_b)[:, :, None, :], pos)[:, :, 0]
    wi = wi * ((IDX_HEADS * IDX_DIM) ** -0.5)
    if page_table is None:
        topk = min(TOPK_MAX, T // 4)
        o = map_query_blocks(lambda pb, qb, qib, wib: dsa_attend(pb, qb, qib, wib, k, v, ki, topk), pos, q, qi, wi)
    else:
        past = gather_pages(past_kv, page_table)
        k_all = jnp.concatenate([past[:, :, 0], k], axis=1)
        v_all = jnp.concatenate([past[:, :, 1], v], axis=1)
        ki_all = jnp.concatenate([gather_pages(past_idx_k, page_table), ki], axis=1)
        topk = min(TOPK_MAX, k_all.shape[1] // 4)
        o = dsa_attend(pos, q, qi, wi, k_all, v_all, ki_all, topk)
    return o @ w_out, (jnp.stack([k, v], axis=2), ki)


def sq_relu_ffn(x, w1, w2):
    return jnp.square(jax.nn.relu(x @ w1)) @ w2


def setup_inputs(seed: int = 0) -> dict:
    key = jax.random.key(seed)
    ks = jax.random.split(key, 40)

    def nrm(i, shape, scale):
        return jax.random.normal(ks[i], shape, jnp.float32) * scale

    n_pages = PAST_LEN // PAGE_SIZE
    n_used = DEC_BATCH * n_pages
    n_phys = n_used + n_used // 4
    page_table = jax.random.permutation(ks[0], n_phys)[:n_used].reshape(DEC_BATCH, n_pages).astype(jnp.int32)
    ab_in = sum(AB_SIZES)
    c_in = sum(C_SIZES)
    u = jax.random.uniform(ks[1], (N_FOX_LAYERS, RNN_WIDTH), jnp.float32, 0.9, 0.999)
    a0 = u ** (1.0 / LRU_C)
    lam = jnp.log(a0) - jnp.log1p(-a0)
    return {
        'x_prompt': nrm(2, (BATCH, SEQ, D_MODEL), 1.0),
        'x_sample': nrm(3, (DEC_BATCH, DEC_SEQ, D_MODEL), 1.0),
        'cache_fox_kv': nrm(4, (N_FOX_LAYERS, n_phys, PAGE_SIZE, 2, FOX_HEADS, HEAD_DIM), 1.0),
        'cache_fox_logf': jax.nn.log_sigmoid(nrm(5, (N_FOX_LAYERS, n_phys, PAGE_SIZE, FOX_HEADS), 1.0)),
        'state_lru_conv': nrm(6, (N_FOX_LAYERS, DEC_BATCH, CONV_W - 1, RNN_WIDTH), 1.0),
        'state_lru_h': nrm(7, (N_FOX_LAYERS, DEC_BATCH, RNN_WIDTH), 0.5),
        'cache_dsa_kv': nrm(8, (N_DSA_LAYERS, n_phys, PAGE_SIZE, 2, DSA_KV_HEADS, HEAD_DIM), 1.0),
        'cache_dsa_idx_k': nrm(9, (N_DSA_LAYERS, n_phys, PAGE_SIZE, IDX_DIM), 1.0),
        'page_table': page_table,
        'norm_mix': 1.0 + nrm(10, (DEPTH, D_MODEL), 0.02),
        'norm_ffn': 1.0 + nrm(11, (DEPTH, D_MODEL), 0.02),
        'norm_final': 1.0 + nrm(12, (D_MODEL,), 0.02),
        'ab_w_in': nrm(13, (N_FOX_LAYERS, D_MODEL, ab_in), D_MODEL ** -0.5),
        'ab_b_f': nrm(14, (N_FOX_LAYERS, FOX_HEADS), 0.02),
        'ab_conv_w': nrm(15, (N_FOX_LAYERS, CONV_W, RNN_WIDTH), CONV_W ** -0.5),
        'ab_conv_b': nrm(16, (N_FOX_LAYERS, RNN_WIDTH), 0.02),
        'ab_gate_a_w': nrm(17, (N_FOX_LAYERS, RNN_BLOCKS, RNN_BLOCK_W, RNN_BLOCK_W), RNN_BLOCK_W ** -0.5),
        'ab_gate_a_b': nrm(18, (N_FOX_LAYERS, RNN_WIDTH), 0.02),
        'ab_gate_x_w': nrm(19, (N_FOX_LAYERS, RNN_BLOCKS, RNN_BLOCK_W, RNN_BLOCK_W), RNN_BLOCK_W ** -0.5),
        'ab_gate_x_b': nrm(20, (N_FOX_LAYERS, RNN_WIDTH), 0.02),
        'ab_lambda': lam,
        'ab_w_out': nrm(21, (N_FOX_LAYERS, FOX_WIDTH + RNN_WIDTH, D_MODEL), (FOX_WIDTH + RNN_WIDTH) ** -0.5),
        'c_w_in': nrm(22, (N_DSA_LAYERS, D_MODEL, c_in), D_MODEL ** -0.5),
        'c_idx_norm_g': 1.0 + nrm(23, (N_DSA_LAYERS, IDX_DIM), 0.02),
        'c_idx_norm_b': nrm(24, (N_DSA_LAYERS, IDX_DIM), 0.02),
        'c_w_out': nrm(25, (N_DSA_LAYERS, DSA_HEADS * HEAD_DIM, D_MODEL), (DSA_HEADS * HEAD_DIM) ** -0.5),
        'ffn_w1': nrm(26, (DEPTH, D_MODEL, D_FF), D_MODEL ** -0.5),
        'ffn_w2': nrm(27, (DEPTH, D_FF, D_MODEL), D_FF ** -0.5),
    }


def reference(x_prompt, x_sample, cache_fox_kv, cache_fox_logf, state_lru_conv, state_lru_h,
              cache_dsa_kv, cache_dsa_idx_k, page_table, norm_mix, norm_ffn, norm_final,
              ab_w_in, ab_b_f, ab_conv_w, ab_conv_b, ab_gate_a_w, ab_gate_a_b, ab_gate_x_w, ab_gate_x_b,
              ab_lambda, ab_w_out, c_w_in, c_idx_norm_g, c_idx_norm_b, c_w_out, ffn_w1, ffn_w2):
    pos_p = jnp.arange(SEQ, dtype=jnp.int32)
    pos_s = PAST_LEN + jnp.arange(DEC_SEQ, dtype=jnp.int32)
    hp, hs = x_prompt, x_sample
    fox_p, fox_s, dsa_p, dsa_s = [], [], [], []
    for layer in range(DEPTH):
        li = layer // 2
        if layer % 2 == 0:
            w = (ab_w_in[li], ab_b_f[li], ab_conv_w[li], ab_conv_b[li], ab_gate_a_w[li], ab_gate_a_b[li],
                 ab_gate_x_w[li], ab_gate_x_b[li], ab_lambda[li], ab_w_out[li])
            conv0 = jnp.zeros((hp.shape[0], CONV_W - 1, RNN_WIDTH), hp.dtype)
            h0 = jnp.zeros((hp.shape[0], RNN_WIDTH), jnp.float32)
            out, st = fox_lru_mixer(rms_norm(hp, norm_mix[layer]), pos_p, conv0, h0, None, None, None, *w)
            hp = hp + out
            fox_p.append(st)
            out, st = fox_lru_mixer(rms_norm(hs, norm_mix[layer]), pos_s, state_lru_conv[li], state_lru_h[li],
                                    cache_fox_kv[li], cache_fox_logf[li], page_table, *w)
            hs = hs + out
            fox_s.append(st)
        else:
            w = (c_w_in[li], c_idx_norm_g[li], c_idx_norm_b[li], c_w_out[li])
            out, st = dsa_mixer(rms_norm(hp, norm_mix[layer]), pos_p, None, None, None, *w)
            hp = hp + out
            dsa_p.append(st)
            out, st = dsa_mixer(rms_norm(hs, norm_mix[layer]), pos_s, cache_dsa_kv[li], cache_dsa_idx_k[li],
                                page_table, *w)
            hs = hs + out
            dsa_s.append(st)
        hp = hp + sq_relu_ffn(rms_norm(hp, norm_ffn[layer]), ffn_w1[layer], ffn_w2[layer])
        hs = hs + sq_relu_ffn(rms_norm(hs, norm_ffn[layer]), ffn_w1[layer], ffn_w2[layer])
    y_prompt = rms_norm(hp, norm_final)
    y_sample = rms_norm(hs, norm_final)
    fox_kv_p, fox_logf_p, lru_conv_p, lru_h_p = (jnp.stack(a) for a in zip(*fox_p))
    fox_kv_s, fox_logf_s, lru_conv_s, lru_h_s = (jnp.stack(a) for a in zip(*fox_s))
    dsa_kv_p, dsa_idxk_p = (jnp.stack(a) for a in zip(*dsa_p))
    dsa_kv_s, dsa_idxk_s = (jnp.stack(a) for a in zip(*dsa_s))
    return (y_prompt, y_sample, fox_kv_p, fox_logf_p, lru_conv_p, lru_h_p, dsa_kv_p, dsa_idxk_p,
            fox_kv_s, fox_logf_s, lru_conv_s, lru_h_s, dsa_kv_s, dsa_idxk_s)
```

```python
import functools

import jax
import jax.numpy as jnp
import numpy as np
from jax import lax
from jax.experimental import pallas as pl
from jax.experimental.pallas import tpu as pltpu

F32 = jnp.float32
BF16 = jnp.bfloat16
I32 = jnp.int32

EPS = 1e-6
HEAD_DIM = 64
ROT_DIM = HEAD_DIM // 4
ROPE_THETA = 500000.0
LRU_C = 8.0
CONV_W = 4
TOPK_MAX = 256
PAGE = 128
LANES = 128
SUBLANES = 8
NEG = -1e30
VMEM_LIMIT = 56 * 1024 * 1024

NT = (((1,), (1,)), ((), ()))


def _dot(a, b):
    return jnp.dot(a, b, preferred_element_type=F32)


def _dot_nt(a, b):
    return lax.dot_general(a, b, NT, preferred_element_type=F32)


def _split3(x):
    hi = x.astype(BF16)
    r1 = x - hi.astype(F32)
    mid = r1.astype(BF16)
    lo = (r1 - mid.astype(F32)).astype(BF16)
    return hi, mid, lo


def _dot_exact_l(e, x):
    hi, mid, lo = _split3(x)
    return _dot(e, hi) + _dot(e, mid) + _dot(e, lo)


def _dot_exact_r(x, e):
    hi, mid, lo = _split3(x)
    return _dot(hi, e) + _dot(mid, e) + _dot(lo, e)


def _rms(x, g):
    return x * lax.rsqrt(jnp.mean(x * x, axis=-1, keepdims=True) + EPS) * g


def _sigmoid(x):
    return 1.0 / (1.0 + jnp.exp(-x))


def _log_sigmoid(x):
    return jnp.minimum(x, 0.0) - jnp.log1p(jnp.exp(-jnp.abs(x)))


def _softplus(x):
    return jnp.maximum(x, 0.0) + jnp.log1p(jnp.exp(-jnp.abs(x)))


def _gelu_tanh(x):
    c = np.float32(np.sqrt(2.0 / np.pi))
    return 0.5 * x * (1.0 + jnp.tanh(c * (x + 0.044715 * (x * x * x))))


def _iota(shape, dim):
    return lax.broadcasted_iota(I32, shape, dim)


def _head_of_lane(shape, dim):
    return lax.shift_right_logical(_iota(shape, dim), HEAD_DIM.bit_length() - 1)


def _params(sem):
    return pltpu.CompilerParams(dimension_semantics=sem, vmem_limit_bytes=VMEM_LIMIT)


def _row_tile(n, pref):
    t = min(n, pref)
    assert n % t == 0
    return t


def _fox_in_kernel(h_ref, g_ref, w_ref, wf_ref, wft_ref, bfc_ref, bfr_ref,
                   q_ref, kv_ref, kvb_ref, xr_ref, gate_ref, logf_ref, ccol_ref, crow_ref,
                   carry_c, carry_r, *, tm, tiles_per_batch, fw, rw):
    i = pl.program_id(0)
    xn = _rms(h_ref[...], g_ref[...]).astype(BF16)
    q = _dot(xn, w_ref[:, 0:fw])
    q_ref[...] = (q * (HEAD_DIM ** -0.5)).astype(BF16)
    kv = _dot(xn, w_ref[:, fw:3 * fw])
    kv_ref[...] = kv
    kvb_ref[...] = kv.astype(BF16)
    xr_ref[...] = _dot(xn, w_ref[:, 3 * fw:3 * fw + rw])
    gate_ref[...] = _dot(xn, w_ref[:, 3 * fw + rw:3 * fw + 2 * rw])

    logf_col = _log_sigmoid(_dot(xn, wf_ref[...]) + bfc_ref[...])
    logf_row = _log_sigmoid(_dot_nt(wft_ref[...], xn) + bfr_ref[...])
    logf_ref[...] = logf_col[:, 0:SUBLANES]

    @pl.when(i % tiles_per_batch == 0)
    def _():
        carry_c[...] = jnp.zeros_like(carry_c)
        carry_r[...] = jnp.zeros_like(carry_r)

    r = _iota((tm, tm), 0)
    c = _iota((tm, tm), 1)
    incl = jnp.where(c <= r, 1.0, 0.0).astype(BF16)
    c_col = _dot_exact_l(incl, logf_col) + carry_c[...]
    incl_t = jnp.where(r <= c, 1.0, 0.0).astype(BF16)
    c_row = _dot_exact_r(logf_row, incl_t) + carry_r[...]
    carry_c[...] = c_col[tm - 1:tm, :]
    carry_r[...] = c_row[:, tm - 1:tm]
    ccol_ref[...] = c_col[:, 0:SUBLANES]
    crow_ref[...] = c_row


def _fox_in(h, g, w_main, wf, wft, bfc, bfr, *, batch, seq, fw, rw):
    n, d = h.shape
    tm = _row_tile(seq, 256)
    tpb = seq // tm
    nt = n // tm
    kern = functools.partial(_fox_in_kernel, tm=tm, tiles_per_batch=tpb, fw=fw, rw=rw)
    const = lambda i: (0, 0)
    row = lambda i: (i, 0)
    return pl.pallas_call(
        kern,
        grid=(nt,),
        in_specs=[
            pl.BlockSpec((tm, d), row),
            pl.BlockSpec((1, d), const),
            pl.BlockSpec(w_main.shape, const),
            pl.BlockSpec(wf.shape, const),
            pl.BlockSpec(wft.shape, const),
            pl.BlockSpec(bfc.shape, const),
            pl.BlockSpec(bfr.shape, const),
        ],
        out_specs=[
            pl.BlockSpec((tm, fw), row),
            pl.BlockSpec((tm, 2 * fw), row),
            pl.BlockSpec((tm, 2 * fw), row),
            pl.BlockSpec((tm, rw), row),
            pl.BlockSpec((tm, rw), row),
            pl.BlockSpec((tm, SUBLANES), row),
            pl.BlockSpec((tm, SUBLANES), row),
            pl.BlockSpec((None, SUBLANES, tm), lambda i: (i // tpb, 0, i % tpb)),
        ],
        out_shape=[
            jax.ShapeDtypeStruct((n, fw), BF16),
            jax.ShapeDtypeStruct((n, 2 * fw), F32),
            jax.ShapeDtypeStruct((n, 2 * fw), BF16),
            jax.ShapeDtypeStruct((n, rw), F32),
            jax.ShapeDtypeStruct((n, rw), F32),
            jax.ShapeDtypeStruct((n, SUBLANES), F32),
            jax.ShapeDtypeStruct((n, SUBLANES), F32),
            jax.ShapeDtypeStruct((batch, SUBLANES, seq), F32),
        ],
        scratch_shapes=[pltpu.VMEM((1, LANES), F32), pltpu.VMEM((SUBLANES, 1), F32)],
        compiler_params=_params(("arbitrary",)),
        name="fox_in_proj",
    )(h, g, w_main, wf, wft, bfc, bfr)


def _fox_attn_kernel(q_ref, kvb_ref, ccol_ref, crow_ref, o_ref, *, tq, tk, n_heads, fw):
    qi = pl.program_id(1)
    low = _iota((tq, LANES), 1) < HEAD_DIM
    n_full = (qi * tq) // tk
    qpos = qi * tq + _iota((tq, tk), 0)
    kloc = _iota((tq, tk), 1)

    for hp in range(n_heads // 2):
        lanes = slice(hp * LANES, (hp + 1) * LANES)
        vlanes = slice(fw + hp * LANES, fw + (hp + 1) * LANES)
        qp = q_ref[:, lanes]
        halves = []
        for sub in range(2):
            h = 2 * hp + sub
            qm = jnp.where(low if sub == 0 else jnp.logical_not(low), qp, jnp.zeros_like(qp))
            cq = ccol_ref[:, h:h + 1]

            def tile(kt, carry, masked, qm=qm, cq=cq, h=h, lanes=lanes, vlanes=vlanes):
                m, l, acc = carry
                rows = pl.ds(pl.multiple_of(kt * tk, tk), tk)
                k = kvb_ref[rows, lanes]
                v = kvb_ref[rows, vlanes]
                ck = crow_ref[h, pl.ds(kt, 1), :]
                s = _dot_nt(qm, k) + (cq - ck)
                if masked:
                    s = jnp.where(kt * tk + kloc <= qpos, s, NEG)
                m_new = jnp.maximum(m, jnp.max(s, axis=-1, keepdims=True))
                alpha = jnp.exp(m - m_new)
                p = jnp.exp(s - m_new)
                l = alpha * l + jnp.sum(p, axis=-1, keepdims=True)
                acc = alpha * acc + _dot(p.astype(BF16), v)
                return m_new, l, acc

            init = (jnp.full((tq, 1), NEG, F32), jnp.zeros((tq, 1), F32), jnp.zeros((tq, LANES), F32))
            carry = lax.fori_loop(0, n_full, functools.partial(tile, masked=False), init)
            m, l, acc = tile(n_full, carry, True)
            halves.append(acc / l)
        o_ref[:, lanes] = jnp.where(low, halves[0], halves[1]).astype(BF16)


def _fox_attn(q, kvb, ccol, crow, *, n_heads):
    b, t, fw = q.shape
    tq = _row_tile(t, 128)
    tk = _row_tile(t, 256)
    crow4 = crow.reshape(b, SUBLANES, t // tk, tk)
    kern = functools.partial(_fox_attn_kernel, tq=tq, tk=tk, n_heads=n_heads, fw=fw)
    return pl.pallas_call(
        kern,
        grid=(b, t // tq),
        in_specs=[
            pl.BlockSpec((None, tq, fw), lambda bi, qi: (bi, qi, 0)),
            pl.BlockSpec((None, t, 2 * fw), lambda bi, qi: (bi, 0, 0)),
            pl.BlockSpec((None, tq, SUBLANES), lambda bi, qi: (bi, qi, 0)),
            pl.BlockSpec((None, SUBLANES, t // tk, tk), lambda bi, qi: (bi, 0, 0, 0)),
        ],
        out_specs=pl.BlockSpec((None, tq, fw), lambda bi, qi: (bi, qi, 0)),
        out_shape=jax.ShapeDtypeStruct((b, t, fw), BF16),
        compiler_params=_params(("arbitrary", "arbitrary")),
        name="fox_attn",
    )(q, kvb, ccol, crow4)


def _lru_gates(xc, ga_ref, gab_ref, gx_ref, gxb_ref, sp):
    xcb = xc.astype(BF16)
    r = _sigmoid(_dot(xcb, ga_ref[...]) + gab_ref[...])
    i = _sigmoid(_dot(xcb, gx_ref[...]) + gxb_ref[...])
    log_a = -LRU_C * r * sp
    a = jnp.exp(log_a)
    u = jnp.sqrt(-jnp.tanh(log_a) * (a * a + 1.0)) * (i * xc)
    return a, u


def _lru_kernel(xr_ref, gate_ref, cw_ref, cb_ref, ga_ref, gab_ref, gx_ref, gxb_ref, lam_ref,
                yg_ref, ht_ref, xbuf, abuf, ubuf, hcar, *, tt):
    t = pl.program_id(1)
    rw = xr_ref.shape[-1]

    @pl.when(t == 0)
    def _():
        xbuf[0:SUBLANES, :] = jnp.zeros((SUBLANES, rw), F32)
        hcar[...] = jnp.zeros_like(hcar)

    x = xr_ref[...]
    xbuf[SUBLANES:SUBLANES + tt, :] = x
    x1 = xbuf[SUBLANES - 1:SUBLANES - 1 + tt, :]
    x2 = xbuf[SUBLANES - 2:SUBLANES - 2 + tt, :]
    x3 = xbuf[SUBLANES - 3:SUBLANES - 3 + tt, :]
    xc = cb_ref[...] + (cw_ref[0:1, :] * x3 + cw_ref[1:2, :] * x2 + cw_ref[2:3, :] * x1 + cw_ref[3:4, :] * x)
    xbuf[0:SUBLANES, :] = x[tt - SUBLANES:tt, :]

    a, u = _lru_gates(xc, ga_ref, gab_ref, gx_ref, gxb_ref, _softplus(-lam_ref[...]))
    abuf[...] = a
    ubuf[...] = u

    row = _iota((SUBLANES, rw), 0)

    def group(gi, hc):
        rows = pl.ds(pl.multiple_of(gi * SUBLANES, SUBLANES), SUBLANES)
        aa = abuf[rows, :]
        uu = ubuf[rows, :]
        for d in (1, 2, 4):
            keep = row >= d
            a_sh = jnp.where(keep, pltpu.roll(aa, d, axis=0), 1.0)
            u_sh = jnp.where(keep, pltpu.roll(uu, d, axis=0), 0.0)
            uu = aa * u_sh + uu
            aa = aa * a_sh
        hh = uu + aa * hc
        ubuf[rows, :] = hh
        return hh[SUBLANES - 1:SUBLANES, :]

    hc = lax.fori_loop(0, tt // SUBLANES, group, hcar[...])
    hcar[...] = hc
    ht_ref[...] = hc
    yg_ref[...] = (ubuf[...] * _gelu_tanh(gate_ref[...])).astype(BF16)


def _lru(xr, gate, cw, cb, ga, gab, gx, gxb, lam):
    b, t, rw = xr.shape
    tt = _row_tile(t, 256)
    const = lambda bi, ti: (0, 0)
    blk = pl.BlockSpec((None, tt, rw), lambda bi, ti: (bi, ti, 0))
    return pl.pallas_call(
        functools.partial(_lru_kernel, tt=tt),
        grid=(b, t // tt),
        in_specs=[blk, blk,
                  pl.BlockSpec(cw.shape, const), pl.BlockSpec(cb.shape, const),
                  pl.BlockSpec(ga.shape, const), pl.BlockSpec(gab.shape, const),
                  pl.BlockSpec(gx.shape, const), pl.BlockSpec(gxb.shape, const),
                  pl.BlockSpec(lam.shape, const)],
        out_specs=[blk, pl.BlockSpec((None, 1, rw), lambda bi, ti: (bi, 0, 0))],
        out_shape=[jax.ShapeDtypeStruct((b, t, rw), BF16), jax.ShapeDtypeStruct((b, 1, rw), F32)],
        scratch_shapes=[pltpu.VMEM((SUBLANES + tt, rw), F32), pltpu.VMEM((tt, rw), F32),
                        pltpu.VMEM((tt, rw), F32), pltpu.VMEM((1, rw), F32)],
        compiler_params=_params(("arbitrary", "arbitrary")),
        name="rglru",
    )(xr, gate, cw, cb, ga, gab, gx, gxb, lam)


def _lru_step_kernel(xr_ref, gate_ref, cp_ref, h0_ref, cw_ref, cb_ref, ga_ref, gab_ref, gx_ref, gxb_ref,
                     lam_ref, yg_ref, ht_ref):
    x = xr_ref[...]
    xc = cb_ref[...] + (cw_ref[0:1, :] * cp_ref[0] + cw_ref[1:2, :] * cp_ref[1] + cw_ref[2:3, :] * cp_ref[2]
                        + cw_ref[3:4, :] * x)
    a, u = _lru_gates(xc, ga_ref, gab_ref, gx_ref, gxb_ref, _softplus(-lam_ref[...]))
    h = a * h0_ref[...] + u
    ht_ref[...] = h
    yg_ref[...] = (h * _gelu_tanh(gate_ref[...])).astype(BF16)


def _lru_step(xr, gate, cp, h0, cw, cb, ga, gab, gx, gxb, lam):
    n, rw = xr.shape
    args = (xr, gate, cp, h0, cw, cb, ga, gab, gx, gxb, lam)
    return pl.pallas_call(
        _lru_step_kernel,
        grid=(1,),
        in_specs=[pl.BlockSpec(a.shape, lambda i, nd=a.ndim: (0,) * nd) for a in args],
        out_specs=[pl.BlockSpec((n, rw), lambda i: (0, 0)), pl.BlockSpec((n, rw), lambda i: (0, 0))],
        out_shape=[jax.ShapeDtypeStruct((n, rw), BF16), jax.ShapeDtypeStruct((n, rw), F32)],
        compiler_params=_params(("arbitrary",)),
        name="rglru_step",
    )(*args)


def _out_ffn_kernel(*refs, n_pieces, pair_major, final, ff_chunk):
    pieces = refs[:n_pieces]
    h_ref, wout_ref, g_ref, w1_ref, w2_ref = refs[n_pieces:n_pieces + 5]
    gf_ref = refs[n_pieces + 5] if final else None
    out_ref = refs[-1]

    mix = None
    off = 0
    for p in pieces:
        if pair_major:
            a = jnp.concatenate([p[j] for j in range(p.shape[0])], axis=-1)
        else:
            a = p[...]
        k = a.shape[-1]
        part = _dot(a, wout_ref[off:off + k, :])
        mix = part if mix is None else mix + part
        off += k
    h1 = h_ref[...] + mix
    xn = _rms(h1, g_ref[...]).astype(BF16)
    d_ff = w1_ref.shape[1]
    ffn = None
    for c in range(d_ff // ff_chunk):
        cols = slice(c * ff_chunk, (c + 1) * ff_chunk)
        hm = jnp.maximum(_dot(xn, w1_ref[:, cols]), 0.0)
        part = _dot((hm * hm).astype(BF16), w2_ref[cols, :])
        ffn = part if ffn is None else ffn + part
    y = h1 + ffn
    if final:
        y = _rms(y, gf_ref[...])
    out_ref[...] = y


def _out_ffn(pieces, h, wout, g, w1, w2, gf=None, *, pair_major=False, seq=None):
    n, d = h.shape
    tm = _row_tile(n if seq is None else seq, 256)
    tpb = None if seq is None else seq // tm
    const = lambda i: (0, 0)
    row = lambda i: (i, 0)
    in_specs = []
    for p in pieces:
        if pair_major:
            in_specs.append(pl.BlockSpec((None, p.shape[1], tm, p.shape[3]), lambda i: (i // tpb, 0, i % tpb, 0)))
        else:
            in_specs.append(pl.BlockSpec((tm, p.shape[1]), row))
    in_specs += [pl.BlockSpec((tm, d), row), pl.BlockSpec(wout.shape, const), pl.BlockSpec(g.shape, const),
                 pl.BlockSpec(w1.shape, const), pl.BlockSpec(w2.shape, const)]
    args = list(pieces) + [h, wout, g, w1, w2]
    if gf is not None:
        in_specs.append(pl.BlockSpec(gf.shape, const))
        args.append(gf)
    kern = functools.partial(_out_ffn_kernel, n_pieces=len(pieces), pair_major=pair_major,
                             final=gf is not None, ff_chunk=min(512, w1.shape[1]))
    return pl.pallas_call(
        kern,
        grid=(n // tm,),
        in_specs=in_specs,
        out_specs=pl.BlockSpec((tm, d), row),
        out_shape=jax.ShapeDtypeStruct((n, d), F32),
        compiler_params=_params(("arbitrary",)),
        name="out_proj_ffn",
    )(*args)


def _rope(x, cos, sn, sp):
    return x * cos + pltpu.roll(x, LANES - ROT_DIM // 2, axis=1) * sn + pltpu.roll(x, ROT_DIM // 2, axis=1) * sp


def _dup_halves(x, low):
    a = jnp.where(low, x, 0.0)
    b = x - a
    return a + pltpu.roll(a, HEAD_DIM, axis=1), b + pltpu.roll(b, HEAD_DIM, axis=1)


def _dsa_in_kernel(h_ref, g_ref, wq_ref, wkv_ref, wqi_ref, wkw_ref, cos_ref, sn_ref, sp_ref, ng_ref, nb_ref,
                   q_ref, kv_ref, kdup_ref, vdup_ref, qi_ref, ki_ref, kidup_ref, wi_ref, *, tm, wi_scale):
    xn = _rms(h_ref[...], g_ref[...]).astype(BF16)
    cos, sn, sp = cos_ref[...], sn_ref[...], sp_ref[...]
    low = _iota((tm, LANES), 1) < HEAD_DIM
    nq = wq_ref.shape[1] // LANES
    nkv = wkv_ref.shape[1] // (2 * LANES)
    nqi = wqi_ref.shape[1] // LANES

    for j in range(nq):
        x = _dot(xn, wq_ref[:, j * LANES:(j + 1) * LANES])
        q_ref[j] = (_rope(x, cos, sn, sp) * (HEAD_DIM ** -0.5)).astype(BF16)
    for j in range(nkv):
        k = _rope(_dot(xn, wkv_ref[:, j * LANES:(j + 1) * LANES]), cos, sn, sp)
        kv_ref[:, j * LANES:(j + 1) * LANES] = k
        ka, kb = _dup_halves(k, low)
        kdup_ref[2 * j] = ka.astype(BF16)
        kdup_ref[2 * j + 1] = kb.astype(BF16)
        v = _dot(xn, wkv_ref[:, (nkv + j) * LANES:(nkv + j + 1) * LANES])
        kv_ref[:, (nkv + j) * LANES:(nkv + j + 1) * LANES] = v
        va, vb = _dup_halves(v, low)
        vdup_ref[2 * j] = va.astype(BF16)
        vdup_ref[2 * j + 1] = vb.astype(BF16)
    for j in range(nqi):
        x = _dot(xn, wqi_ref[:, j * LANES:(j + 1) * LANES])
        qi_ref[:, j * LANES:(j + 1) * LANES] = _rope(x, cos, sn, sp).astype(BF16)

    kw = _dot(xn, wkw_ref[...])
    inv_n = 1.0 / HEAD_DIM
    mu = jnp.sum(jnp.where(low, kw, 0.0), axis=-1, keepdims=True) * inv_n
    cen = jnp.where(low, kw - mu, 0.0)
    var = jnp.sum(cen * cen, axis=-1, keepdims=True) * inv_n
    ki = _rope(cen * lax.rsqrt(var + EPS) * ng_ref[...] + nb_ref[...], cos, sn, sp)
    ki_ref[...] = ki[:, 0:HEAD_DIM]
    kidup_ref[...] = (ki + pltpu.roll(ki, HEAD_DIM, axis=1)).astype(BF16)
    wi_ref[...] = jnp.where(low, 0.0, kw * wi_scale)


def _dsa_in(h, g, wq, wkv, wqi, wkw, cos, sn, sp, ng, nb, *, batch, seq, wi_scale):
    n, d = h.shape
    tm = _row_tile(seq, 256)
    tpb = seq // tm
    nq, nkvp, qiw = wq.shape[1] // LANES, wkv.shape[1] // LANES, wqi.shape[1]
    const = lambda i: (0, 0)
    row = lambda i: (i, 0)
    pm = lambda i: (i // tpb, 0, i % tpb, 0)
    tab = pl.BlockSpec((tm, LANES), (lambda i: (i % tpb, 0)) if cos.shape[0] == seq else row)
    kern = functools.partial(_dsa_in_kernel, tm=tm, wi_scale=wi_scale)
    return pl.pallas_call(
        kern,
        grid=(n // tm,),
        in_specs=[pl.BlockSpec((tm, d), row), pl.BlockSpec(g.shape, const), pl.BlockSpec(wq.shape, const),
                  pl.BlockSpec(wkv.shape, const), pl.BlockSpec(wqi.shape, const), pl.BlockSpec(wkw.shape, const),
                  tab, tab, tab, pl.BlockSpec(ng.shape, const), pl.BlockSpec(nb.shape, const)],
        out_specs=[
            pl.BlockSpec((None, nq, tm, LANES), pm),
            pl.BlockSpec((tm, wkv.shape[1]), row),
            pl.BlockSpec((None, nkvp, tm, LANES), pm),
            pl.BlockSpec((None, nkvp, tm, LANES), pm),
            pl.BlockSpec((tm, qiw), row),
            pl.BlockSpec((tm, HEAD_DIM), row),
            pl.BlockSpec((tm, LANES), row),
            pl.BlockSpec((tm, LANES), row),
        ],
        out_shape=[
            jax.ShapeDtypeStruct((batch, nq, seq, LANES), BF16),
            jax.ShapeDtypeStruct((n, wkv.shape[1]), F32),
            jax.ShapeDtypeStruct((batch, nkvp, seq, LANES), BF16),
            jax.ShapeDtypeStruct((batch, nkvp, seq, LANES), BF16),
            jax.ShapeDtypeStruct((n, qiw), BF16),
            jax.ShapeDtypeStruct((n, HEAD_DIM), F32),
            jax.ShapeDtypeStruct((n, LANES), BF16),
            jax.ShapeDtypeStruct((n, LANES), F32),
        ],
        compiler_params=_params(("arbitrary",)),
        name="dsa_in_proj",
    )(h, g, wq, wkv, wqi, wkw, cos, sn, sp, ng, nb)


def _ordered_bits(x):
    return x ^ ((x >> 31) & jnp.int32(0x7FFFFFFF))


def _topk_bias(sc_ref, bias_ref, valid_fn, *, k, n_tiles, rows, width):
    int_min = jnp.int32(-2 ** 31)

    def count(pred):
        tot = jnp.zeros((rows, width), F32)
        for t in range(n_tiles):
            tot = tot + jnp.where(pred(sc_ref[t]), 1.0, 0.0)
        return jnp.sum(tot, axis=-1, keepdims=True)

    def to_float(key_u):
        return lax.bitcast_convert_type(_ordered_bits(key_u ^ int_min), F32)

    def bit_step(it, key_u):
        cand = key_u | lax.shift_left(jnp.int32(1), 31 - it)
        cnt = count(lambda s, c=to_float(cand): s >= c)
        return jnp.where(cnt >= k, cand, key_u)

    tau = to_float(lax.fori_loop(0, 32, bit_step, jnp.zeros((rows, 1), I32)))
    tau = jnp.where(count(lambda s: s > -jnp.inf) >= k, tau, -jnp.inf)
    need = k - count(lambda s: s > tau)

    r = _iota((width, width), 0)
    c = _iota((width, width), 1)
    before = jnp.where(r < c, 1.0, 0.0).astype(BF16)
    seen = jnp.zeros((rows, 1), F32)
    for t in range(n_tiles):
        s = sc_ref[t]
        eq = s == tau
        eqf = jnp.where(eq, 1.0, 0.0)
        rank = _dot(eqf.astype(BF16), before) + seen
        sel = jnp.logical_and(jnp.logical_or(s > tau, jnp.logical_and(eq, rank < need)), valid_fn(t))
        bias_ref[t] = jnp.where(sel, 0.0, NEG)
        seen = seen + jnp.sum(eqf, axis=-1, keepdims=True)


def _indexer_scores(qi_ref, wi, kid, low, n_pairs):
    sc = None
    for p in range(n_pairs):
        qp = qi_ref[:, p * LANES:(p + 1) * LANES]
        for sub in range(2):
            hh = 2 * p + sub
            qm = jnp.where(low if sub == 0 else jnp.logical_not(low), qp, jnp.zeros_like(qp))
            term = jnp.maximum(_dot_nt(qm, kid), 0.0) * wi[:, HEAD_DIM + hh:HEAD_DIM + hh + 1]
            sc = term if sc is None else sc + term
    return sc


def _dsa_attn_kernel(qi_ref, wi_ref, q_ref, kid_ref, kdup_ref, vdup_ref, o_ref, sc_ref, bias_ref,
                     *, tq, tk, n_tiles, topk, n_idx_pairs, n_kv, group):
    qt = pl.program_id(1)
    low = _iota((tq, LANES), 1) < HEAD_DIM
    n_full = (qt * tq) // tk
    qpos = qt * tq + _iota((tq, tk), 0)
    kloc = _iota((tq, tk), 1)
    wi = wi_ref[...]

    for t in range(n_tiles):
        sc_ref[t] = jnp.full((tq, tk), -jnp.inf, F32)

    def score_tile(kt, masked):
        rows = pl.ds(pl.multiple_of(kt * tk, tk), tk)
        sc = _indexer_scores(qi_ref, wi, kid_ref[rows, :], low, n_idx_pairs)
        if masked:
            sc = jnp.where(kt * tk + kloc <= qpos, sc, -jnp.inf)
        sc_ref[kt] = sc

    def score_body(kt, carry):
        score_tile(kt, False)
        return carry

    lax.fori_loop(0, n_full, score_body, 0)
    score_tile(n_full, True)

    _topk_bias(sc_ref, bias_ref, lambda t: t * tk + kloc <= qpos, k=topk, n_tiles=n_tiles, rows=tq, width=tk)

    pairs_per_kv = group // 2
    for n in range(n_kv):
        parts = []
        for j in range(pairs_per_kv):
            qp = q_ref[n * pairs_per_kv + j]
            parts.append(jnp.where(low, qp, jnp.zeros_like(qp)))
            parts.append(jnp.where(low, jnp.zeros_like(qp), qp))
        qs = jnp.concatenate(parts, axis=0)

        def tile(kt, carry, n=n, qs=qs):
            m, l, acc = carry
            rows = pl.ds(pl.multiple_of(kt * tk, tk), tk)
            s = _dot_nt(qs, kdup_ref[n, rows, :])
            s = (s.reshape(group, tq, tk) + bias_ref[kt][None]).reshape(group * tq, tk)
            m_new = jnp.maximum(m, jnp.max(s, axis=-1, keepdims=True))
            alpha = jnp.exp(m - m_new)
            p = jnp.exp(s - m_new)
            l = alpha * l + jnp.sum(p, axis=-1, keepdims=True)
            acc = alpha * acc + _dot(p.astype(BF16), vdup_ref[n, rows, :])
            return m_new, l, acc

        gm = group * tq
        init = (jnp.full((gm, 1), NEG, F32), jnp.zeros((gm, 1), F32), jnp.zeros((gm, LANES), F32))
        m, l, acc = lax.fori_loop(0, n_full + 1, tile, init)
        o = acc / l
        for j in range(pairs_per_kv):
            lo = o[(2 * j) * tq:(2 * j + 1) * tq]
            hi = o[(2 * j + 1) * tq:(2 * j + 2) * tq]
            o_ref[n * pairs_per_kv + j] = jnp.where(low, lo, hi).astype(BF16)


def _dsa_attn(qi, wi, q, kid, kdup, vdup, *, topk):
    b, t, qiw = qi.shape
    nq = q.shape[1]
    nkvp = kdup.shape[1]
    tq = _row_tile(t, 128)
    tk = _row_tile(t, 256)
    n_tiles = t // tk
    kern = functools.partial(_dsa_attn_kernel, tq=tq, tk=tk, n_tiles=n_tiles, topk=topk,
                             n_idx_pairs=qiw // LANES, n_kv=nkvp, group=2 * nq // nkvp)
    return pl.pallas_call(
        kern,
        grid=(b, t // tq),
        in_specs=[
            pl.BlockSpec((None, tq, qiw), lambda bi, qt: (bi, qt, 0)),
            pl.BlockSpec((None, tq, LANES), lambda bi, qt: (bi, qt, 0)),
            pl.BlockSpec((None, nq, tq, LANES), lambda bi, qt: (bi, 0, qt, 0)),
            pl.BlockSpec((None, t, LANES), lambda bi, qt: (bi, 0, 0)),
            pl.BlockSpec((None, nkvp, t, LANES), lambda bi, qt: (bi, 0, 0, 0)),
            pl.BlockSpec((None, nkvp, t, LANES), lambda bi, qt: (bi, 0, 0, 0)),
        ],
        out_specs=pl.BlockSpec((None, nq, tq, LANES), lambda bi, qt: (bi, 0, qt, 0)),
        out_shape=jax.ShapeDtypeStruct((b, nq, t, LANES), BF16),
        scratch_shapes=[pltpu.VMEM((n_tiles, tq, tk), F32), pltpu.VMEM((n_tiles, tq, tk), F32)],
        compiler_params=_params(("arbitrary", "arbitrary")),
        name="dsa_attn",
    )(qi, wi, q, kid, kdup, vdup)


def _fox_dec_kernel(pt_ref, q_ref, kvn_ref, lfn_ref, kvc_ref, lfc_ref, o_ref, qblk, m_s, l_s, acc_s, car_s,
                    *, n_pages, fw):
    p = pl.program_id(1)
    nh = SUBLANES
    diag = _head_of_lane((nh, fw), 1) == _iota((nh, fw), 0)

    @pl.when(p == 0)
    def _():
        qb = jnp.where(diag, jnp.broadcast_to(q_ref[...].astype(F32), (nh, fw)), 0.0)
        qblk[...] = qb.astype(BF16)
        kn = kvn_ref[:, 0:fw]
        vn = kvn_ref[:, fw:2 * fw]
        m_s[...] = jnp.sum(qb * kn, axis=-1, keepdims=True)
        l_s[...] = jnp.ones_like(l_s)
        acc_s[...] = jnp.broadcast_to(vn, (nh, fw))
        car_s[...] = lfn_ref[...]

    k = kvc_ref[:, 0:fw].astype(BF16)
    v = kvc_ref[:, fw:2 * fw].astype(BF16)
    lf = lfc_ref[...]
    r = _iota((PAGE, PAGE), 0)
    c = _iota((PAGE, PAGE), 1)
    after = jnp.where(r > c, 1.0, 0.0).astype(BF16)
    car = car_s[...]
    s = _dot_nt(qblk[...], k) + (_dot_exact_r(lf, after) + car)
    car_s[...] = car + jnp.sum(lf, axis=-1, keepdims=True)
    m = m_s[...]
    m_new = jnp.maximum(m, jnp.max(s, axis=-1, keepdims=True))
    alpha = jnp.exp(m - m_new)
    pr = jnp.exp(s - m_new)
    l_s[...] = alpha * l_s[...] + jnp.sum(pr, axis=-1, keepdims=True)
    acc_s[...] = alpha * acc_s[...] + _dot(pr.astype(BF16), v)
    m_s[...] = m_new

    @pl.when(p == n_pages - 1)
    def _():
        o = jnp.where(diag, acc_s[...] / l_s[...], 0.0)
        o_ref[...] = jnp.sum(o, axis=0, keepdims=True).astype(BF16)


def _fox_dec(page_table, q, kvn, lfn, kvc, lfc):
    n, _, fw = q.shape
    n_pages = page_table.shape[1]
    kern = functools.partial(_fox_dec_kernel, n_pages=n_pages, fw=fw)
    page = lambda b, p, pt: (pt[b, n_pages - 1 - p], 0, 0)
    samp = lambda b, p, pt: (b, 0, 0)
    grid_spec = pltpu.PrefetchScalarGridSpec(
        num_scalar_prefetch=1,
        grid=(n, n_pages),
        in_specs=[
            pl.BlockSpec((None, 1, fw), samp),
            pl.BlockSpec((None, 1, 2 * fw), samp),
            pl.BlockSpec((None, SUBLANES, 1), samp),
            pl.BlockSpec((None, PAGE, 2 * fw), page),
            pl.BlockSpec((None, SUBLANES, PAGE), page),
        ],
        out_specs=pl.BlockSpec((None, 1, fw), samp),
        scratch_shapes=[pltpu.VMEM((SUBLANES, fw), BF16), pltpu.VMEM((SUBLANES, 1), F32),
                        pltpu.VMEM((SUBLANES, 1), F32), pltpu.VMEM((SUBLANES, fw), F32),
                        pltpu.VMEM((SUBLANES, 1), F32)],
    )
    return pl.pallas_call(
        kern,
        grid_spec=grid_spec,
        out_shape=jax.ShapeDtypeStruct((n, 1, fw), BF16),
        compiler_params=_params(("arbitrary", "arbitrary")),
        name="fox_decode_attn",
    )(page_table, q, kvn, lfn, kvc, lfc)


def _dsa_dec_score_kernel(pt_ref, qi_ref, wi_ref, kin_ref, kic_ref, sc_ref, *, n_pages):
    p = pl.program_id(1)
    qi = qi_ref[...]
    wi = wi_ref[...]

    @pl.when(p < n_pages)
    def _():
        d = _dot_nt(qi, kic_ref[...].astype(BF16))
        sc_ref[...] = jnp.sum(jnp.maximum(d, 0.0) * wi, axis=0, keepdims=True)

    @pl.when(p == n_pages)
    def _():
        kn = kin_ref[...].astype(BF16).astype(F32)
        d = jnp.sum(qi.astype(F32) * kn, axis=-1, keepdims=True)
        s_new = jnp.sum(jnp.maximum(d, 0.0) * wi, axis=0, keepdims=True)
        sc_ref[...] = jnp.where(_iota((1, PAGE), 1) == 0, s_new, -jnp.inf)


def _dsa_dec_scores(page_table, qi, wi, kin, kic):
    n, nh, _ = qi.shape
    n_pages = page_table.shape[1]
    page = lambda b, p, pt: (pt[b, jnp.minimum(p, n_pages - 1)], 0, 0)
    samp = lambda b, p, pt: (b, 0, 0)
    grid_spec = pltpu.PrefetchScalarGridSpec(
        num_scalar_prefetch=1,
        grid=(n, n_pages + 1),
        in_specs=[
            pl.BlockSpec((None, nh, HEAD_DIM), samp),
            pl.BlockSpec((None, nh, 1), samp),
            pl.BlockSpec((None, 1, HEAD_DIM), samp),
            pl.BlockSpec((None, PAGE, HEAD_DIM), page),
        ],
        out_specs=pl.BlockSpec((None, None, 1, PAGE), lambda b, p, pt: (p, b, 0, 0)),
    )
    return pl.pallas_call(
        functools.partial(_dsa_dec_score_kernel, n_pages=n_pages),
        grid_spec=grid_spec,
        out_shape=jax.ShapeDtypeStruct((n_pages + 1, n, 1, PAGE), F32),
        compiler_params=_params(("arbitrary", "arbitrary")),
        name="dsa_decode_scores",
    )(page_table, qi, wi, kin, kic)


def _dsa_dec_select_kernel(sc_ref, bias_ref, *, topk, n_tiles, rows, n_keys):
    lane = _iota((rows, PAGE), 1)
    _topk_bias(sc_ref, bias_ref, lambda t: t * PAGE + lane < n_keys, k=topk, n_tiles=n_tiles, rows=rows,
               width=PAGE)


def _dsa_dec_select(scores, *, topk, n_keys):
    n_tiles, rows, _ = scores.shape
    spec = pl.BlockSpec(scores.shape, lambda i: (0, 0, 0))
    return pl.pallas_call(
        functools.partial(_dsa_dec_select_kernel, topk=topk, n_tiles=n_tiles, rows=rows, n_keys=n_keys),
        grid=(1,),
        in_specs=[spec],
        out_specs=spec,
        out_shape=jax.ShapeDtypeStruct(scores.shape, F32),
        compiler_params=_params(("arbitrary",)),
        name="dsa_decode_select",
    )(scores)


def _dsa_dec_attn_kernel(pt_ref, q_ref, kvn_ref, bias_ref, kvc_ref, o_ref, qblk, m_s, l_s, acc_s,
                         *, n_pages, kw, group):
    p = pl.program_id(1)
    nh = q_ref.shape[0]
    n_kv = kw // HEAD_DIM
    own = _head_of_lane((nh, kw), 1) == lax.shift_right_logical(_iota((nh, kw), 0), group.bit_length() - 1)
    r = _iota((HEAD_DIM, kw), 0)
    c = _iota((HEAD_DIM, kw), 1)
    spread = jnp.where((c & (HEAD_DIM - 1)) == r, 1.0, 0.0).astype(BF16)

    @pl.when(p == 0)
    def _():
        qb = jnp.where(own, _dot(q_ref[...], spread), 0.0)
        qblk[...] = qb.astype(BF16)
        m_s[...] = jnp.full_like(m_s, NEG)
        l_s[...] = jnp.zeros_like(l_s)
        acc_s[...] = jnp.zeros_like(acc_s)

    def update(s, pv_fn):
        m = m_s[...]
        m_new = jnp.maximum(m, jnp.max(s, axis=-1, keepdims=True))
        alpha = jnp.exp(m - m_new)
        pr = jnp.exp(s - m_new)
        l_s[...] = alpha * l_s[...] + jnp.sum(pr, axis=-1, keepdims=True)
        acc_s[...] = alpha * acc_s[...] + pv_fn(pr)
        m_s[...] = m_new

    b = bias_ref[...]

    @pl.when(p < n_pages)
    def _():
        k = kvc_ref[:, 0:kw].astype(BF16)
        v = kvc_ref[:, kw:2 * kw].astype(BF16)
        update(_dot_nt(qblk[...], k) + b, lambda pr: _dot(pr.astype(BF16), v))

    @pl.when(p == n_pages)
    def _():
        kn = kvn_ref[:, 0:kw].astype(BF16).astype(F32)
        vn = kvn_ref[:, kw:2 * kw].astype(BF16).astype(F32)
        s_new = jnp.sum(qblk[...].astype(F32) * kn, axis=-1, keepdims=True) + b[:, 0:1]
        update(s_new, lambda pr: pr * vn)
        o = jnp.where(own, acc_s[...] / l_s[...], 0.0).astype(BF16)
        o_ref[...] = lax.dot_general(o, spread, NT, preferred_element_type=F32).astype(BF16)


def _dsa_dec_attn(page_table, q, kvn, bias, kvc, *, group):
    n, nh, _ = q.shape
    kw = kvn.shape[-1] // 2
    n_pages = page_table.shape[1]
    page = lambda b, p, pt: (pt[b, jnp.minimum(p, n_pages - 1)], 0, 0)
    samp = lambda b, p, pt: (b, 0, 0)
    grid_spec = pltpu.PrefetchScalarGridSpec(
        num_scalar_prefetch=1,
        grid=(n, n_pages + 1),
        in_specs=[
            pl.BlockSpec((None, nh, HEAD_DIM), samp),
            pl.BlockSpec((None, 1, 2 * kw), samp),
            pl.BlockSpec((None, None, 1, PAGE), lambda b, p, pt: (p, b, 0, 0)),
            pl.BlockSpec((None, PAGE, 2 * kw), page),
        ],
        out_specs=pl.BlockSpec((None, nh, HEAD_DIM), samp),
        scratch_shapes=[pltpu.VMEM((nh, kw), BF16), pltpu.VMEM((nh, 1), F32), pltpu.VMEM((nh, 1), F32),
                        pltpu.VMEM((nh, kw), F32)],
    )
    return pl.pallas_call(
        functools.partial(_dsa_dec_attn_kernel, n_pages=n_pages, kw=kw, group=group),
        grid_spec=grid_spec,
        out_shape=jax.ShapeDtypeStruct((n, nh, HEAD_DIM), BF16),
        compiler_params=_params(("arbitrary", "arbitrary")),
        name="dsa_decode_attn",
    )(page_table, q, kvn, bias, kvc)


def _block_diag(w):
    nb, bw, _ = w.shape
    eye = jnp.eye(nb, dtype=w.dtype)
    return jnp.einsum("nde,nm->ndme", w, eye).reshape(nb * bw, nb * bw)


def _rope_tables(pos):
    half = ROT_DIM // 2
    inv = ROPE_THETA ** (-jnp.arange(half, dtype=F32) / half)
    ang = pos.astype(F32)[:, None] * inv[None, :]
    cos, sin = jnp.cos(ang), jnp.sin(ang)
    rest = jnp.zeros((pos.shape[0], HEAD_DIM - ROT_DIM), F32)
    z = jnp.zeros_like(sin)
    cos_t = jnp.concatenate([cos, cos, rest + 1.0], axis=-1)
    sn_t = jnp.concatenate([-sin, z, rest], axis=-1)
    sp_t = jnp.concatenate([z, sin, rest], axis=-1)
    twice = lambda a: jnp.concatenate([a, a], axis=-1)
    return twice(cos_t), twice(sn_t), twice(sp_t)


def _pad_cols(w, width):
    return jnp.pad(w, ((0, 0), (0, width - w.shape[1])))


def kernel(x_prompt, x_sample, cache_fox_kv, cache_fox_logf, state_lru_conv, state_lru_h, cache_dsa_kv,
           cache_dsa_idx_k, page_table, norm_mix, norm_ffn, norm_final, ab_w_in, ab_b_f, ab_conv_w, ab_conv_b,
           ab_gate_a_w, ab_gate_a_b, ab_gate_x_w, ab_gate_x_b, ab_lambda, ab_w_out, c_w_in, c_idx_norm_g,
           c_idx_norm_b, c_w_out, ffn_w1, ffn_w2):
    batch, seq, d_model = x_prompt.shape
    n_dec = x_sample.shape[0]
    assert x_sample.shape[1] == 1
    n_pages = page_table.shape[1]
    past_len = n_pages * PAGE
    fox_heads = ab_b_f.shape[1]
    assert fox_heads == SUBLANES
    fw = fox_heads * HEAD_DIM
    rw = ab_lambda.shape[1]
    kvw = cache_dsa_kv.shape[4] * HEAD_DIM
    idx_heads = c_w_in.shape[2] - (d_model + 2 * kvw + HEAD_DIM)
    idx_heads = idx_heads // (HEAD_DIM + 1)
    qiw = idx_heads * HEAD_DIM
    dsa_heads = d_model // HEAD_DIM
    group = dsa_heads // (kvw // HEAD_DIM)
    assert ab_w_in.shape[2] == 3 * fw + fox_heads + 2 * rw
    assert c_w_in.shape[2] == d_model + 2 * kvw + qiw + HEAD_DIM + idx_heads

    w = ab_w_in[0]
    ab_main = jnp.concatenate([w[:, 0:3 * fw], w[:, 3 * fw + fox_heads:]], axis=1).astype(BF16)
    w_f = w[:, 3 * fw:3 * fw + fox_heads]
    ab_wf = _pad_cols(w_f, LANES).astype(BF16)
    ab_wft = w_f.T.astype(BF16)
    bf_col = _pad_cols(ab_b_f, LANES)
    bf_row = ab_b_f.reshape(fox_heads, 1)
    ga = _block_diag(ab_gate_a_w[0]).astype(BF16)
    gx = _block_diag(ab_gate_x_w[0]).astype(BF16)
    lru_w = (ab_conv_w[0], ab_conv_b, ga, ab_gate_a_b, gx, ab_gate_x_b, ab_lambda)
    ab_wout = ab_w_out[0].astype(BF16)
    cw = c_w_in[0]
    c_wq = cw[:, 0:d_model].astype(BF16)
    c_wkv = cw[:, d_model:d_model + 2 * kvw].astype(BF16)
    c_wqi = cw[:, d_model + 2 * kvw:d_model + 2 * kvw + qiw].astype(BF16)
    c_wkw = _pad_cols(cw[:, d_model + 2 * kvw + qiw:], LANES).astype(BF16)
    ng = _pad_cols(c_idx_norm_g, LANES)
    nb = _pad_cols(c_idx_norm_b, LANES)
    c_wout = c_w_out[0].astype(BF16)
    w1 = ffn_w1.astype(BF16)
    w2 = ffn_w2.astype(BF16)
    g_mix0, g_mix1 = norm_mix[0:1], norm_mix[1:2]
    g_ffn0, g_ffn1 = norm_ffn[0:1], norm_ffn[1:2]
    g_fin = norm_final.reshape(1, d_model)
    wi_scale = float(qiw) ** -0.5

    n_p = batch * seq
    h0 = x_prompt.reshape(n_p, d_model)
    q, kv, kvb, xr, gate, logf, ccol, crow = _fox_in(h0, g_mix0, ab_main, ab_wf, ab_wft, bf_col, bf_row,
                                                     batch=batch, seq=seq, fw=fw, rw=rw)
    attn = _fox_attn(q.reshape(batch, seq, fw), kvb.reshape(batch, seq, 2 * fw),
                     ccol.reshape(batch, seq, fox_heads), crow, n_heads=fox_heads)
    xr3 = xr.reshape(batch, seq, rw)
    yg, h_t = _lru(xr3, gate.reshape(batch, seq, rw), *lru_w)
    h1 = _out_ffn([attn.reshape(n_p, fw), yg.reshape(n_p, rw)], h0, ab_wout, g_ffn0, w1[0], w2[0])

    cos_p, sn_p, sp_p = _rope_tables(jnp.arange(seq, dtype=I32))
    q2, kv2, kdup, vdup, qi, ki, kidup, wi = _dsa_in(h1, g_mix1, c_wq, c_wkv, c_wqi, c_wkw, cos_p, sn_p, sp_p,
                                                     ng, nb, batch=batch, seq=seq, wi_scale=wi_scale)
    o2 = _dsa_attn(qi.reshape(batch, seq, qiw), wi.reshape(batch, seq, LANES), q2,
                   kidup.reshape(batch, seq, LANES), kdup, vdup, topk=min(TOPK_MAX, seq // 4))
    y_prompt = _out_ffn([o2], h1, c_wout, g_ffn1, w1[1], w2[1], g_fin, pair_major=True, seq=seq)

    hs0 = x_sample.reshape(n_dec, d_model)
    qs, kvs, _, xrs, gates, logfs, _, _ = _fox_in(hs0, g_mix0, ab_main, ab_wf, ab_wft, bf_col, bf_row,
                                                  batch=1, seq=n_dec, fw=fw, rw=rw)
    kvc = cache_fox_kv[0].reshape(-1, PAGE, 2 * fw)
    lfc = jnp.swapaxes(cache_fox_logf[0], 1, 2)
    attn_s = _fox_dec(page_table, qs.reshape(n_dec, 1, fw), kvs.reshape(n_dec, 1, 2 * fw),
                      logfs.reshape(n_dec, fox_heads, 1), kvc, lfc)
    conv_prev = jnp.swapaxes(state_lru_conv[0], 0, 1)
    ygs, h_ts = _lru_step(xrs, gates, conv_prev, state_lru_h[0], *lru_w)
    hs1 = _out_ffn([attn_s.reshape(n_dec, fw), ygs], hs0, ab_wout, g_ffn0, w1[0], w2[0])

    pos_s = jnp.full((n_dec,), past_len, I32)
    cos_s, sn_s, sp_s = _rope_tables(pos_s)
    q2s, kv2s, _, _, qis, kis, _, wis = _dsa_in(hs1, g_mix1, c_wq, c_wkv, c_wqi, c_wkw, cos_s, sn_s, sp_s,
                                                ng, nb, batch=1, seq=n_dec, wi_scale=wi_scale)
    q2s = jnp.swapaxes(q2s[0], 0, 1).reshape(n_dec, dsa_heads, HEAD_DIM)
    wis_h = wis[:, HEAD_DIM:HEAD_DIM + idx_heads].reshape(n_dec, idx_heads, 1)
    scores = _dsa_dec_scores(page_table, qis.reshape(n_dec, idx_heads, HEAD_DIM), wis_h,
                             kis.reshape(n_dec, 1, HEAD_DIM), cache_dsa_idx_k[0])
    n_keys = past_len + 1
    bias = _dsa_dec_select(scores.reshape(n_pages + 1, n_dec, PAGE), topk=min(TOPK_MAX, n_keys // 4),
                           n_keys=n_keys)
    o2s = _dsa_dec_attn(page_table, q2s, kv2s.reshape(n_dec, 1, 2 * kvw),
                        bias.reshape(n_pages + 1, n_dec, 1, PAGE), cache_dsa_kv[0].reshape(-1, PAGE, 2 * kvw),
                        group=group)
    y_sample = _out_ffn([o2s.reshape(n_dec, d_model)], hs1, c_wout, g_ffn1, w1[1], w2[1], g_fin)

    kv_heads = kvw // HEAD_DIM
    return (
        y_prompt.reshape(batch, seq, d_model),
        y_sample.reshape(n_dec, 1, d_model),
        kv.reshape(1, batch, seq, 2, fox_heads, HEAD_DIM),
        logf.reshape(1, batch, seq, fox_heads),
        xr3[:, seq - (CONV_W - 1):, :][None],
        h_t.reshape(1, batch, rw),
        kv2.reshape(1, batch, seq, 2, kv_heads, HEAD_DIM),
        ki.reshape(1, batch, seq, HEAD_DIM),
        kvs.reshape(1, n_dec, 1, 2, fox_heads, HEAD_DIM),
        logfs.reshape(1, n_dec, 1, fox_heads),
        jnp.concatenate([state_lru_conv[0][:, 1:, :], xrs[:, None, :]], axis=1)[None],
        h_ts.reshape(1, n_dec, rw),
        kv2s.reshape(1, n_dec, 1, 2, kv_heads, HEAD_DIM),
        kis.reshape(1, n_dec, 1, HEAD_DIM),
    )
```

```python
import functools

import jax
import jax.numpy as jnp
import numpy as np
from jax import lax
from jax.experimental import pallas as pl
from jax.experimental.pallas import tpu as pltpu

F32 = jnp.float32
BF16 = jnp.bfloat16
I32 = jnp.int32

EPS = 1e-6
HEAD_DIM = 64
ROT_DIM = HEAD_DIM // 4
ROPE_THETA = 500000.0
LRU_C = 8.0
CONV_W = 4
TOPK_MAX = 256
PAGE = 128
LANES = 128
SUBLANES = 8
NEG = -1e30
VMEM_LIMIT = 56 * 1024 * 1024

NT = (((1,), (1,)), ((), ()))


def _dot(a, b):
    return jnp.dot(a, b, preferred_element_type=F32)


def _dot_nt(a, b):
    return lax.dot_general(a, b, NT, preferred_element_type=F32)


def _split3(x):
    hi = x.astype(BF16)
    r1 = x - hi.astype(F32)
    mid = r1.astype(BF16)
    lo = (r1 - mid.astype(F32)).astype(BF16)
    return hi, mid, lo


def _dot_exact_l(e, x):
    hi, mid, lo = _split3(x)
    return _dot(e, hi) + _dot(e, mid) + _dot(e, lo)


def _dot_exact_r(x, e):
    hi, mid, lo = _split3(x)
    return _dot(hi, e) + _dot(mid, e) + _dot(lo, e)


def _rms(x, g):
    return x * lax.rsqrt(jnp.mean(x * x, axis=-1, keepdims=True) + EPS) * g


def _sigmoid(x):
    return 1.0 / (1.0 + jnp.exp(-x))


def _log_sigmoid(x):
    return jnp.minimum(x, 0.0) - jnp.log1p(jnp.exp(-jnp.abs(x)))


def _softplus(x):
    return jnp.maximum(x, 0.0) + jnp.log1p(jnp.exp(-jnp.abs(x)))


def _gelu_tanh(x):
    c = np.float32(np.sqrt(2.0 / np.pi))
    return 0.5 * x * (1.0 + jnp.tanh(c * (x + 0.044715 * (x * x * x))))


def _iota(shape, dim):
    return lax.broadcasted_iota(I32, shape, dim)


def _head_of_lane(shape, dim):
    return lax.shift_right_logical(_iota(shape, dim), HEAD_DIM.bit_length() - 1)


def _params(sem):
    return pltpu.CompilerParams(dimension_semantics=sem, vmem_limit_bytes=VMEM_LIMIT)


def _row_tile(n, pref):
    t = min(n, pref)
    assert n % t == 0
    return t


def _fox_in_kernel(h_ref, g_ref, w_ref, wf_ref, wft_ref, bfc_ref, bfr_ref,
                   q_ref, kv_ref, kvb_ref, xr_ref, gate_ref, logf_ref, ccol_ref, crow_ref,
                   carry_c, carry_r, *, tm, tiles_per_batch, fw, rw):
    i = pl.program_id(0)
    xn = _rms(h_ref[...], g_ref[...]).astype(BF16)
    q = _dot(xn, w_ref[:, 0:fw])
    q_ref[...] = (q * (HEAD_DIM ** -0.5)).astype(BF16)
    kv = _dot(xn, w_ref[:, fw:3 * fw])
    kv_ref[...] = kv
    kvb_ref[...] = kv.astype(BF16)
    xr_ref[...] = _dot(xn, w_ref[:, 3 * fw:3 * fw + rw])
    gate_ref[...] = _dot(xn, w_ref[:, 3 * fw + rw:3 * fw + 2 * rw])

    logf_col = _log_sigmoid(_dot(xn, wf_ref[...]) + bfc_ref[...])
    logf_row = _log_sigmoid(_dot_nt(wft_ref[...], xn) + bfr_ref[...])
    logf_ref[...] = logf_col[:, 0:SUBLANES]

    @pl.when(i % tiles_per_batch == 0)
    def _():
        carry_c[...] = jnp.zeros_like(carry_c)
        carry_r[...] = jnp.zeros_like(carry_r)

    r = _iota((tm, tm), 0)
    c = _iota((tm, tm), 1)
    incl = jnp.where(c <= r, 1.0, 0.0).astype(BF16)
    c_col = _dot_exact_l(incl, logf_col) + carry_c[...]
    incl_t = jnp.where(r <= c, 1.0, 0.0).astype(BF16)
    c_row = _dot_exact_r(logf_row, incl_t) + carry_r[...]
    carry_c[...] = c_col[tm - 1:tm, :]
    carry_r[...] = c_row[:, tm - 1:tm]
    ccol_ref[...] = c_col[:, 0:SUBLANES]
    crow_ref[...] = c_row


def _fox_in(h, g, w_main, wf, wft, bfc, bfr, *, batch, seq, fw, rw):
    n, d = h.shape
    tm = _row_tile(seq, 256)
    tpb = seq // tm
    nt = n // tm
    kern = functools.partial(_fox_in_kernel, tm=tm, tiles_per_batch=tpb, fw=fw, rw=rw)
    const = lambda i: (0, 0)
    row = lambda i: (i, 0)
    return pl.pallas_call(
        kern,
        grid=(nt,),
        in_specs=[
            pl.BlockSpec((tm, d), row),
            pl.BlockSpec((1, d), const),
            pl.BlockSpec(w_main.shape, const),
            pl.BlockSpec(wf.shape, const),
            pl.BlockSpec(wft.shape, const),
            pl.BlockSpec(bfc.shape, const),
            pl.BlockSpec(bfr.shape, const),
        ],
        out_specs=[
            pl.BlockSpec((tm, fw), row),
            pl.BlockSpec((tm, 2 * fw), row),
            pl.BlockSpec((tm, 2 * fw), row),
            pl.BlockSpec((tm, rw), row),
            pl.BlockSpec((tm, rw), row),
            pl.BlockSpec((tm, SUBLANES), row),
            pl.BlockSpec((tm, SUBLANES), row),
            pl.BlockSpec((None, SUBLANES, tm), lambda i: (i // tpb, 0, i % tpb)),
        ],
        out_shape=[
            jax.ShapeDtypeStruct((n, fw), BF16),
            jax.ShapeDtypeStruct((n, 2 * fw), F32),
            jax.ShapeDtypeStruct((n, 2 * fw), BF16),
            jax.ShapeDtypeStruct((n, rw), F32),
            jax.ShapeDtypeStruct((n, rw), F32),
            jax.ShapeDtypeStruct((n, SUBLANES), F32),
            jax.ShapeDtypeStruct((n, SUBLANES), F32),
            jax.ShapeDtypeStruct((batch, SUBLANES, seq), F32),
        ],
        scratch_shapes=[pltpu.VMEM((1, LANES), F32), pltpu.VMEM((SUBLANES, 1), F32)],
        compiler_params=_params(("arbitrary",)),
        name="fox_in_proj",
    )(h, g, w_main, wf, wft, bfc, bfr)


def _fox_attn_kernel(q_ref, kvb_ref, ccol_ref, crow_ref, o_ref, qm_ref, m_ref, l_ref, acc_ref,
                     *, tq, tk, n_heads, fw):
    qi = pl.program_id(1)
    low = _iota((tq, LANES), 1) < HEAD_DIM
    n_full = (qi * tq) // tk
    qpos = qi * tq + _iota((tq, tk), 0)
    kloc = _iota((tq, tk), 1)

    for h in range(n_heads):
        qp = q_ref[:, (h // 2) * LANES:(h // 2 + 1) * LANES]
        qm_ref[h] = jnp.where(low if h % 2 == 0 else jnp.logical_not(low), qp, jnp.zeros_like(qp))
    m_ref[...] = jnp.full_like(m_ref, NEG)
    l_ref[...] = jnp.zeros_like(l_ref)
    acc_ref[...] = jnp.zeros_like(acc_ref)

    def tile(kt, masked):
        rows = pl.ds(pl.multiple_of(kt * tk, tk), tk)
        for h in range(n_heads):
            hp = h // 2
            k = kvb_ref[rows, hp * LANES:(hp + 1) * LANES]
            v = kvb_ref[rows, fw + hp * LANES:fw + (hp + 1) * LANES]
            ck = crow_ref[h, pl.ds(kt, 1), :]
            s = _dot_nt(qm_ref[h], k) + (ccol_ref[:, h:h + 1] - ck)
            if masked:
                s = jnp.where(kt * tk + kloc <= qpos, s, NEG)
            m = m_ref[h]
            m_new = jnp.maximum(m, jnp.max(s, axis=-1, keepdims=True))
            alpha = jnp.exp(m - m_new)
            p = jnp.exp(s - m_new)
            l_ref[h] = alpha * l_ref[h] + jnp.sum(p, axis=-1, keepdims=True)
            acc_ref[h] = alpha * acc_ref[h] + _dot(p.astype(BF16), v)
            m_ref[h] = m_new

    def body(kt, carry):
        tile(kt, False)
        return carry

    lax.fori_loop(0, n_full, body, 0)
    tile(n_full, True)
    for hp in range(n_heads // 2):
        o0 = acc_ref[2 * hp] / l_ref[2 * hp]
        o1 = acc_ref[2 * hp + 1] / l_ref[2 * hp + 1]
        o_ref[:, hp * LANES:(hp + 1) * LANES] = jnp.where(low, o0, o1).astype(BF16)


def _fox_attn(q, kvb, ccol, crow, *, n_heads):
    b, t, fw = q.shape
    tq = _row_tile(t, 128)
    tk = _row_tile(t, 256)
    crow4 = crow.reshape(b, SUBLANES, t // tk, tk)
    kern = functools.partial(_fox_attn_kernel, tq=tq, tk=tk, n_heads=n_heads, fw=fw)
    return pl.pallas_call(
        kern,
        grid=(b, t // tq),
        in_specs=[
            pl.BlockSpec((None, tq, fw), lambda bi, qi: (bi, qi, 0)),
            pl.BlockSpec((None, t, 2 * fw), lambda bi, qi: (bi, 0, 0)),
            pl.BlockSpec((None, tq, SUBLANES), lambda bi, qi: (bi, qi, 0)),
            pl.BlockSpec((None, SUBLANES, t // tk, tk), lambda bi, qi: (bi, 0, 0, 0)),
        ],
        out_specs=pl.BlockSpec((None, tq, fw), lambda bi, qi: (bi, qi, 0)),
        out_shape=jax.ShapeDtypeStruct((b, t, fw), BF16),
        scratch_shapes=[pltpu.VMEM((n_heads, tq, LANES), BF16), pltpu.VMEM((n_heads, tq, 1), F32),
                        pltpu.VMEM((n_heads, tq, 1), F32), pltpu.VMEM((n_heads, tq, LANES), F32)],
        compiler_params=_params(("arbitrary", "arbitrary")),
        name="fox_attn",
    )(q, kvb, ccol, crow4)


def _lru_gates(xc, ga_ref, gab_ref, gx_ref, gxb_ref, sp):
    xcb = xc.astype(BF16)
    r = _sigmoid(_dot(xcb, ga_ref[...]) + gab_ref[...])
    i = _sigmoid(_dot(xcb, gx_ref[...]) + gxb_ref[...])
    log_a = -LRU_C * r * sp
    a = jnp.exp(log_a)
    u = jnp.sqrt(-jnp.tanh(log_a) * (a * a + 1.0)) * (i * xc)
    return a, u


def _lru_kernel(xr_ref, gate_ref, cw_ref, cb_ref, ga_ref, gab_ref, gx_ref, gxb_ref, lam_ref,
                yg_ref, ht_ref, xbuf, abuf, ubuf, hcar, *, tt):
    t = pl.program_id(1)
    rw = xr_ref.shape[-1]

    @pl.when(t == 0)
    def _():
        xbuf[0:SUBLANES, :] = jnp.zeros((SUBLANES, rw), F32)
        hcar[...] = jnp.zeros_like(hcar)

    x = xr_ref[...]
    xbuf[SUBLANES:SUBLANES + tt, :] = x
    x1 = xbuf[SUBLANES - 1:SUBLANES - 1 + tt, :]
    x2 = xbuf[SUBLANES - 2:SUBLANES - 2 + tt, :]
    x3 = xbuf[SUBLANES - 3:SUBLANES - 3 + tt, :]
    xc = cb_ref[...] + (cw_ref[0:1, :] * x3 + cw_ref[1:2, :] * x2 + cw_ref[2:3, :] * x1 + cw_ref[3:4, :] * x)
    xbuf[0:SUBLANES, :] = x[tt - SUBLANES:tt, :]

    a, u = _lru_gates(xc, ga_ref, gab_ref, gx_ref, gxb_ref, _softplus(-lam_ref[...]))
    abuf[...] = a
    ubuf[...] = u

    row = _iota((SUBLANES, rw), 0)

    def group(gi, hc):
        rows = pl.ds(pl.multiple_of(gi * SUBLANES, SUBLANES), SUBLANES)
        aa = abuf[rows, :]
        uu = ubuf[rows, :]
        for d in (1, 2, 4):
            keep = row >= d
            a_sh = jnp.where(keep, pltpu.roll(aa, d, axis=0), 1.0)
            u_sh = jnp.where(keep, pltpu.roll(uu, d, axis=0), 0.0)
            uu = aa * u_sh + uu
            aa = aa * a_sh
        hh = uu + aa * hc
        ubuf[rows, :] = hh
        return hh[SUBLANES - 1:SUBLANES, :]

    hc = lax.fori_loop(0, tt // SUBLANES, group, hcar[...])
    hcar[...] = hc
    ht_ref[...] = hc
    yg_ref[...] = (ubuf[...] * _gelu_tanh(gate_ref[...])).astype(BF16)


def _lru(xr, gate, cw, cb, ga, gab, gx, gxb, lam):
    b, t, rw = xr.shape
    tt = _row_tile(t, 256)
    const = lambda bi, ti: (0, 0)
    blk = pl.BlockSpec((None, tt, rw), lambda bi, ti: (bi, ti, 0))
    return pl.pallas_call(
        functools.partial(_lru_kernel, tt=tt),
        grid=(b, t // tt),
        in_specs=[blk, blk,
                  pl.BlockSpec(cw.shape, const), pl.BlockSpec(cb.shape, const),
                  pl.BlockSpec(ga.shape, const), pl.BlockSpec(gab.shape, const),
                  pl.BlockSpec(gx.shape, const), pl.BlockSpec(gxb.shape, const),
                  pl.BlockSpec(lam.shape, const)],
        out_specs=[blk, pl.BlockSpec((None, 1, rw), lambda bi, ti: (bi, 0, 0))],
        out_shape=[jax.ShapeDtypeStruct((b, t, rw), BF16), jax.ShapeDtypeStruct((b, 1, rw), F32)],
        scratch_shapes=[pltpu.VMEM((SUBLANES + tt, rw), F32), pltpu.VMEM((tt, rw), F32),
                        pltpu.VMEM((tt, rw), F32), pltpu.VMEM((1, rw), F32)],
        compiler_params=_params(("arbitrary", "arbitrary")),
        name="rglru",
    )(xr, gate, cw, cb, ga, gab, gx, gxb, lam)


def _lru_step_kernel(xr_ref, gate_ref, cp_ref, h0_ref, cw_ref, cb_ref, ga_ref, gab_ref, gx_ref, gxb_ref,
                     lam_ref, yg_ref, ht_ref):
    x = xr_ref[...]
    xc = cb_ref[...] + (cw_ref[0:1, :] * cp_ref[0] + cw_ref[1:2, :] * cp_ref[1] + cw_ref[2:3, :] * cp_ref[2]
                        + cw_ref[3:4, :] * x)
    a, u = _lru_gates(xc, ga_ref, gab_ref, gx_ref, gxb_ref, _softplus(-lam_ref[...]))
    h = a * h0_ref[...] + u
    ht_ref[...] = h
    yg_ref[...] = (h * _gelu_tanh(gate_ref[...])).astype(BF16)


def _lru_step(xr, gate, cp, h0, cw, cb, ga, gab, gx, gxb, lam):
    n, rw = xr.shape
    args = (xr, gate, cp, h0, cw, cb, ga, gab, gx, gxb, lam)
    return pl.pallas_call(
        _lru_step_kernel,
        grid=(1,),
        in_specs=[pl.BlockSpec(a.shape, lambda i, nd=a.ndim: (0,) * nd) for a in args],
        out_specs=[pl.BlockSpec((n, rw), lambda i: (0, 0)), pl.BlockSpec((n, rw), lambda i: (0, 0))],
        out_shape=[jax.ShapeDtypeStruct((n, rw), BF16), jax.ShapeDtypeStruct((n, rw), F32)],
        compiler_params=_params(("arbitrary",)),
        name="rglru_step",
    )(*args)


def _out_ffn_kernel(*refs, n_pieces, pair_major, final, ff_chunk):
    pieces = refs[:n_pieces]
    h_ref, wout_ref, g_ref, w1_ref, w2_ref = refs[n_pieces:n_pieces + 5]
    gf_ref = refs[n_pieces + 5] if final else None
    out_ref = refs[-1]

    mix = None
    off = 0
    for p in pieces:
        if pair_major:
            a = jnp.concatenate([p[j] for j in range(p.shape[0])], axis=-1)
        else:
            a = p[...]
        k = a.shape[-1]
        part = _dot(a, wout_ref[off:off + k, :])
        mix = part if mix is None else mix + part
        off += k
    h1 = h_ref[...] + mix
    xn = _rms(h1, g_ref[...]).astype(BF16)
    d_ff = w1_ref.shape[1]
    ffn = None
    for c in range(d_ff // ff_chunk):
        cols = slice(c * ff_chunk, (c + 1) * ff_chunk)
        hm = jnp.maximum(_dot(xn, w1_ref[:, cols]), 0.0)
        part = _dot((hm * hm).astype(BF16), w2_ref[cols, :])
        ffn = part if ffn is None else ffn + part
    y = h1 + ffn
    if final:
        y = _rms(y, gf_ref[...])
    out_ref[...] = y


def _out_ffn(pieces, h, wout, g, w1, w2, gf=None, *, pair_major=False, seq=None):
    n, d = h.shape
    tm = _row_tile(n if seq is None else seq, 256)
    tpb = None if seq is None else seq // tm
    const = lambda i: (0, 0)
    row = lambda i: (i, 0)
    in_specs = []
    for p in pieces:
        if pair_major:
            in_specs.append(pl.BlockSpec((None, p.shape[1], tm, p.shape[3]), lambda i: (i // tpb, 0, i % tpb, 0)))
        else:
            in_specs.append(pl.BlockSpec((tm, p.shape[1]), row))
    in_specs += [pl.BlockSpec((tm, d), row), pl.BlockSpec(wout.shape, const), pl.BlockSpec(g.shape, const),
                 pl.BlockSpec(w1.shape, const), pl.BlockSpec(w2.shape, const)]
    args = list(pieces) + [h, wout, g, w1, w2]
    if gf is not None:
        in_specs.append(pl.BlockSpec(gf.shape, const))
        args.append(gf)
    kern = functools.partial(_out_ffn_kernel, n_pieces=len(pieces), pair_major=pair_major,
                             final=gf is not None, ff_chunk=min(512, w1.shape[1]))
    return pl.pallas_call(
        kern,
        grid=(n // tm,),
        in_specs=in_specs,
        out_specs=pl.BlockSpec((tm, d), row),
        out_shape=jax.ShapeDtypeStruct((n, d), F32),
        compiler_params=_params(("arbitrary",)),
        name="out_proj_ffn",
    )(*args)


def _rope(x, cos, sn, sp):
    return x * cos + pltpu.roll(x, LANES - ROT_DIM // 2, axis=1) * sn + pltpu.roll(x, ROT_DIM // 2, axis=1) * sp


def _dup_halves(x, low):
    a = jnp.where(low, x, 0.0)
    b = x - a
    return a + pltpu.roll(a, HEAD_DIM, axis=1), b + pltpu.roll(b, HEAD_DIM, axis=1)


def _dsa_in_kernel(h_ref, g_ref, wq_ref, wkv_ref, wqi_ref, wkw_ref, cos_ref, sn_ref, sp_ref, ng_ref, nb_ref,
                   q_ref, kv_ref, kdup_ref, vdup_ref, qi_ref, ki_ref, kidup_ref, wi_ref, *, tm, wi_scale):
    xn = _rms(h_ref[...], g_ref[...]).astype(BF16)
    cos, sn, sp = cos_ref[...], sn_ref[...], sp_ref[...]
    low = _iota((tm, LANES), 1) < HEAD_DIM
    nq = wq_ref.shape[1] // LANES
    nkv = wkv_ref.shape[1] // (2 * LANES)
    nqi = wqi_ref.shape[1] // LANES

    for j in range(nq):
        x = _dot(xn, wq_ref[:, j * LANES:(j + 1) * LANES])
        q_ref[j] = (_rope(x, cos, sn, sp) * (HEAD_DIM ** -0.5)).astype(BF16)
    for j in range(nkv):
        k = _rope(_dot(xn, wkv_ref[:, j * LANES:(j + 1) * LANES]), cos, sn, sp)
        kv_ref[:, j * LANES:(j + 1) * LANES] = k
        ka, kb = _dup_halves(k, low)
        kdup_ref[2 * j] = ka.astype(BF16)
        kdup_ref[2 * j + 1] = kb.astype(BF16)
        v = _dot(xn, wkv_ref[:, (nkv + j) * LANES:(nkv + j + 1) * LANES])
        kv_ref[:, (nkv + j) * LANES:(nkv + j + 1) * LANES] = v
        va, vb = _dup_halves(v, low)
        vdup_ref[2 * j] = va.astype(BF16)
        vdup_ref[2 * j + 1] = vb.astype(BF16)
    for j in range(nqi):
        x = _dot(xn, wqi_ref[:, j * LANES:(j + 1) * LANES])
        qi_ref[:, j * LANES:(j + 1) * LANES] = _rope(x, cos, sn, sp).astype(BF16)

    kw = _dot(xn, wkw_ref[...])
    inv_n = 1.0 / HEAD_DIM
    mu = jnp.sum(jnp.where(low, kw, 0.0), axis=-1, keepdims=True) * inv_n
    cen = jnp.where(low, kw - mu, 0.0)
    var = jnp.sum(cen * cen, axis=-1, keepdims=True) * inv_n
    ki = _rope(cen * lax.rsqrt(var + EPS) * ng_ref[...] + nb_ref[...], cos, sn, sp)
    ki_ref[...] = ki[:, 0:HEAD_DIM]
    kidup_ref[...] = (ki + pltpu.roll(ki, HEAD_DIM, axis=1)).astype(BF16)
    wi_ref[...] = jnp.where(low, 0.0, kw * wi_scale)


def _dsa_in(h, g, wq, wkv, wqi, wkw, cos, sn, sp, ng, nb, *, batch, seq, wi_scale):
    n, d = h.shape
    tm = _row_tile(seq, 256)
    tpb = seq // tm
    nq, nkvp, qiw = wq.shape[1] // LANES, wkv.shape[1] // LANES, wqi.shape[1]
    const = lambda i: (0, 0)
    row = lambda i: (i, 0)
    pm = lambda i: (i // tpb, 0, i % tpb, 0)
    tab = pl.BlockSpec((tm, LANES), (lambda i: (i % tpb, 0)) if cos.shape[0] == seq else row)
    kern = functools.partial(_dsa_in_kernel, tm=tm, wi_scale=wi_scale)
    return pl.pallas_call(
        kern,
        grid=(n // tm,),
        in_specs=[pl.BlockSpec((tm, d), row), pl.BlockSpec(g.shape, const), pl.BlockSpec(wq.shape, const),
                  pl.BlockSpec(wkv.shape, const), pl.BlockSpec(wqi.shape, const), pl.BlockSpec(wkw.shape, const),
                  tab, tab, tab, pl.BlockSpec(ng.shape, const), pl.BlockSpec(nb.shape, const)],
        out_specs=[
            pl.BlockSpec((None, nq, tm, LANES), pm),
            pl.BlockSpec((tm, wkv.shape[1]), row),
            pl.BlockSpec((None, nkvp, tm, LANES), pm),
            pl.BlockSpec((None, nkvp, tm, LANES), pm),
            pl.BlockSpec((tm, qiw), row),
            pl.BlockSpec((tm, HEAD_DIM), row),
            pl.BlockSpec((tm, LANES), row),
            pl.BlockSpec((tm, LANES), row),
        ],
        out_shape=[
            jax.ShapeDtypeStruct((batch, nq, seq, LANES), BF16),
            jax.ShapeDtypeStruct((n, wkv.shape[1]), F32),
            jax.ShapeDtypeStruct((batch, nkvp, seq, LANES), BF16),
            jax.ShapeDtypeStruct((batch, nkvp, seq, LANES), BF16),
            jax.ShapeDtypeStruct((n, qiw), BF16),
            jax.ShapeDtypeStruct((n, HEAD_DIM), F32),
            jax.ShapeDtypeStruct((n, LANES), BF16),
            jax.ShapeDtypeStruct((n, LANES), F32),
        ],
        compiler_params=_params(("arbitrary",)),
        name="dsa_in_proj",
    )(h, g, wq, wkv, wqi, wkw, cos, sn, sp, ng, nb)


def _ordered_bits(x):
    return x ^ ((x >> 31) & jnp.int32(0x7FFFFFFF))


def _topk_bias(sc_ref, bias_ref, valid_fn, *, k, n_live, search, rows, width):
    int_min = jnp.int32(-2 ** 31)

    def count(pred):
        tot = lax.fori_loop(0, n_live, lambda t, tot: tot + jnp.where(pred(sc_ref[t]), 1.0, 0.0),
                            jnp.zeros((rows, width), F32))
        return jnp.sum(tot, axis=-1, keepdims=True)

    def to_float(key_u):
        return lax.bitcast_convert_type(_ordered_bits(key_u ^ int_min), F32)

    def bit_step(it, key_u):
        cand = key_u | lax.shift_left(jnp.int32(1), 31 - it)
        cnt = count(lambda s, c=to_float(cand): s >= c)
        return jnp.where(cnt >= k, cand, key_u)

    n_steps = jnp.where(search, 32, 0)
    tau = to_float(lax.fori_loop(0, n_steps, bit_step, jnp.zeros((rows, 1), I32)))
    tau = jnp.where(count(lambda s: s > -jnp.inf) > k, tau, -jnp.inf)
    need = k - count(lambda s: s > tau)

    r = _iota((width, width), 0)
    c = _iota((width, width), 1)
    before = jnp.where(r < c, 1.0, 0.0).astype(BF16)

    def write(t, seen):
        s = sc_ref[t]
        eq = s == tau
        eqf = jnp.where(eq, 1.0, 0.0)
        rank = _dot(eqf.astype(BF16), before) + seen
        sel = jnp.logical_and(jnp.logical_or(s > tau, jnp.logical_and(eq, rank < need)), valid_fn(t))
        bias_ref[t] = jnp.where(sel, 0.0, NEG)
        return seen + jnp.sum(eqf, axis=-1, keepdims=True)

    lax.fori_loop(0, n_live, write, jnp.zeros((rows, 1), F32))


def _indexer_scores(qi_ref, wi, kid, low, n_pairs):
    sc = None
    for p in range(n_pairs):
        qp = qi_ref[:, p * LANES:(p + 1) * LANES]
        for sub in range(2):
            hh = 2 * p + sub
            qm = jnp.where(low if sub == 0 else jnp.logical_not(low), qp, jnp.zeros_like(qp))
            term = jnp.maximum(_dot_nt(qm, kid), 0.0) * wi[:, HEAD_DIM + hh:HEAD_DIM + hh + 1]
            sc = term if sc is None else sc + term
    return sc


def _dsa_attn_kernel(qi_ref, wi_ref, q_ref, kid_ref, kdup_ref, vdup_ref, o_ref, sc_ref, bias_ref,
                     qs_ref, m_ref, l_ref, acc_ref, *, tq, tk, topk, n_idx_pairs, n_kv, group):
    qt = pl.program_id(1)
    low = _iota((tq, LANES), 1) < HEAD_DIM
    n_full = (qt * tq) // tk
    qpos = qt * tq + _iota((tq, tk), 0)
    kloc = _iota((tq, tk), 1)
    wi = wi_ref[...]

    def score_tile(kt, masked):
        rows = pl.ds(pl.multiple_of(kt * tk, tk), tk)
        sc = _indexer_scores(qi_ref, wi, kid_ref[rows, :], low, n_idx_pairs)
        if masked:
            sc = jnp.where(kt * tk + kloc <= qpos, sc, -jnp.inf)
        sc_ref[kt] = sc

    def score_body(kt, carry):
        score_tile(kt, False)
        return carry

    lax.fori_loop(0, n_full, score_body, 0)
    score_tile(n_full, True)

    _topk_bias(sc_ref, bias_ref, lambda t: t * tk + kloc <= qpos, k=topk, n_live=n_full + 1,
               search=(qt + 1) * tq > topk, rows=tq, width=tk)

    pairs_per_kv = group // 2
    for n in range(n_kv):
        for j in range(pairs_per_kv):
            qp = q_ref[n * pairs_per_kv + j]
            qs_ref[n, (2 * j) * tq:(2 * j + 1) * tq, :] = jnp.where(low, qp, jnp.zeros_like(qp))
            qs_ref[n, (2 * j + 1) * tq:(2 * j + 2) * tq, :] = jnp.where(low, jnp.zeros_like(qp), qp)
    m_ref[...] = jnp.full_like(m_ref, NEG)
    l_ref[...] = jnp.zeros_like(l_ref)
    acc_ref[...] = jnp.zeros_like(acc_ref)

    def tile(kt, carry):
        rows = pl.ds(pl.multiple_of(kt * tk, tk), tk)
        b = bias_ref[kt]
        for n in range(n_kv):
            s = _dot_nt(qs_ref[n], kdup_ref[n, rows, :])
            s = (s.reshape(group, tq, tk) + b[None]).reshape(group * tq, tk)
            m = m_ref[n]
            m_new = jnp.maximum(m, jnp.max(s, axis=-1, keepdims=True))
            alpha = jnp.exp(m - m_new)
            p = jnp.exp(s - m_new)
            l_ref[n] = alpha * l_ref[n] + jnp.sum(p, axis=-1, keepdims=True)
            acc_ref[n] = alpha * acc_ref[n] + _dot(p.astype(BF16), vdup_ref[n, rows, :])
            m_ref[n] = m_new
        return carry

    lax.fori_loop(0, n_full + 1, tile, 0)
    for n in range(n_kv):
        o = acc_ref[n] / l_ref[n]
        for j in range(pairs_per_kv):
            lo = o[(2 * j) * tq:(2 * j + 1) * tq]
            hi = o[(2 * j + 1) * tq:(2 * j + 2) * tq]
            o_ref[n * pairs_per_kv + j] = jnp.where(low, lo, hi).astype(BF16)


def _dsa_attn(qi, wi, q, kid, kdup, vdup, *, topk):
    b, t, qiw = qi.shape
    nq = q.shape[1]
    nkvp = kdup.shape[1]
    tq = _row_tile(t, 128)
    tk = _row_tile(t, 256)
    n_tiles = t // tk
    group = 2 * nq // nkvp
    kern = functools.partial(_dsa_attn_kernel, tq=tq, tk=tk, topk=topk,
                             n_idx_pairs=qiw // LANES, n_kv=nkvp, group=group)
    return pl.pallas_call(
        kern,
        grid=(b, t // tq),
        in_specs=[
            pl.BlockSpec((None, tq, qiw), lambda bi, qt: (bi, qt, 0)),
            pl.BlockSpec((None, tq, LANES), lambda bi, qt: (bi, qt, 0)),
            pl.BlockSpec((None, nq, tq, LANES), lambda bi, qt: (bi, 0, qt, 0)),
            pl.BlockSpec((None, t, LANES), lambda bi, qt: (bi, 0, 0)),
            pl.BlockSpec((None, nkvp, t, LANES), lambda bi, qt: (bi, 0, 0, 0)),
            pl.BlockSpec((None, nkvp, t, LANES), lambda bi, qt: (bi, 0, 0, 0)),
        ],
        out_specs=pl.BlockSpec((None, nq, tq, LANES), lambda bi, qt: (bi, 0, qt, 0)),
        out_shape=jax.ShapeDtypeStruct((b, nq, t, LANES), BF16),
        scratch_shapes=[pltpu.VMEM((n_tiles, tq, tk), F32), pltpu.VMEM((n_tiles, tq, tk), F32),
                        pltpu.VMEM((nkvp, group * tq, LANES), BF16), pltpu.VMEM((nkvp, group * tq, 1), F32),
                        pltpu.VMEM((nkvp, group * tq, 1), F32), pltpu.VMEM((nkvp, group * tq, LANES), F32)],
        compiler_params=_params(("arbitrary", "arbitrary")),
        name="dsa_attn",
    )(qi, wi, q, kid, kdup, vdup)


def _page_specs(block, n_pages):
    nd = len(block) - 1
    return [pl.BlockSpec(block, lambda b, pt, j=j: (pt[b, j],) + (0,) * nd) for j in range(n_pages)]


def _fox_dec_kernel(pt_ref, q_ref, kvn_ref, lfn_ref, *rest, n_pages, fw):
    kv_refs, lf_refs, o_ref = rest[:n_pages], rest[n_pages:2 * n_pages], rest[2 * n_pages]
    nh = SUBLANES
    diag = _head_of_lane((nh, fw), 1) == _iota((nh, fw), 0)
    qb = jnp.where(diag, jnp.broadcast_to(q_ref[...].astype(F32), (nh, fw)), 0.0)
    qblk = qb.astype(BF16)
    r = _iota((PAGE, PAGE), 0)
    c = _iota((PAGE, PAGE), 1)
    after = jnp.where(r > c, 1.0, 0.0).astype(BF16)

    car = lfn_ref[...]
    s_pages = [None] * n_pages
    for j in reversed(range(n_pages)):
        lf = lf_refs[j][...]
        k = kv_refs[j][:, 0:fw].astype(BF16)
        s_pages[j] = _dot_nt(qblk, k) + (_dot_exact_r(lf, after) + car)
        car = car + jnp.sum(lf, axis=-1, keepdims=True)
    s_new = jnp.sum(qb * kvn_ref[:, 0:fw], axis=-1, keepdims=True)

    m = s_new
    for s in s_pages:
        m = jnp.maximum(m, jnp.max(s, axis=-1, keepdims=True))
    p_new = jnp.exp(s_new - m)
    l = p_new
    acc = p_new * kvn_ref[:, fw:2 * fw]
    for j in range(n_pages):
        pr = jnp.exp(s_pages[j] - m)
        l = l + jnp.sum(pr, axis=-1, keepdims=True)
        acc = acc + _dot(pr.astype(BF16), kv_refs[j][:, fw:2 * fw].astype(BF16))
    o = jnp.where(diag, acc / l, 0.0)
    o_ref[...] = jnp.sum(o, axis=0, keepdims=True).astype(BF16)


def _fox_dec(page_table, q, kvn, lfn, kvc, lfc):
    n, _, fw = q.shape
    n_pages = page_table.shape[1]
    kern = functools.partial(_fox_dec_kernel, n_pages=n_pages, fw=fw)
    samp = lambda b, pt: (b, 0, 0)
    grid_spec = pltpu.PrefetchScalarGridSpec(
        num_scalar_prefetch=1,
        grid=(n,),
        in_specs=[
            pl.BlockSpec((None, 1, fw), samp),
            pl.BlockSpec((None, 1, 2 * fw), samp),
            pl.BlockSpec((None, SUBLANES, 1), samp),
        ] + _page_specs((None, PAGE, 2 * fw), n_pages) + _page_specs((None, SUBLANES, PAGE), n_pages),
        out_specs=pl.BlockSpec((None, 1, fw), samp),
    )
    return pl.pallas_call(
        kern,
        grid_spec=grid_spec,
        out_shape=jax.ShapeDtypeStruct((n, 1, fw), BF16),
        compiler_params=_params(("arbitrary",)),
        name="fox_decode_attn",
    )(page_table, q, kvn, lfn, *([kvc] * n_pages), *([lfc] * n_pages))


def _dsa_dec_score_kernel(pt_ref, qi_ref, wi_ref, kin_ref, *rest, n_pages):
    ki_refs, sc_ref = rest[:n_pages], rest[n_pages]
    qi = qi_ref[...]
    wi = wi_ref[...]
    for j in range(n_pages):
        d = _dot_nt(qi, ki_refs[j][...].astype(BF16))
        sc_ref[j] = jnp.sum(jnp.maximum(d, 0.0) * wi, axis=0, keepdims=True)
    kn = kin_ref[...].astype(BF16).astype(F32)
    d = jnp.sum(qi.astype(F32) * kn, axis=-1, keepdims=True)
    s_new = jnp.sum(jnp.maximum(d, 0.0) * wi, axis=0, keepdims=True)
    sc_ref[n_pages] = jnp.where(_iota((1, PAGE), 1) == 0, s_new, -jnp.inf)


def _dsa_dec_scores(page_table, qi, wi, kin, kic):
    n, nh, _ = qi.shape
    n_pages = page_table.shape[1]
    samp = lambda b, pt: (b, 0, 0)
    grid_spec = pltpu.PrefetchScalarGridSpec(
        num_scalar_prefetch=1,
        grid=(n,),
        in_specs=[
            pl.BlockSpec((None, nh, HEAD_DIM), samp),
            pl.BlockSpec((None, nh, 1), samp),
            pl.BlockSpec((None, 1, HEAD_DIM), samp),
        ] + _page_specs((None, PAGE, HEAD_DIM), n_pages),
        out_specs=pl.BlockSpec((n_pages + 1, None, 1, PAGE), lambda b, pt: (0, b, 0, 0)),
    )
    return pl.pallas_call(
        functools.partial(_dsa_dec_score_kernel, n_pages=n_pages),
        grid_spec=grid_spec,
        out_shape=jax.ShapeDtypeStruct((n_pages + 1, n, 1, PAGE), F32),
        compiler_params=_params(("arbitrary",)),
        name="dsa_decode_scores",
    )(page_table, qi, wi, kin, *([kic] * n_pages))


def _dsa_dec_select_kernel(sc_ref, bias_ref, *, topk, n_tiles, rows, n_keys):
    lane = _iota((rows, PAGE), 1)
    _topk_bias(sc_ref, bias_ref, lambda t: t * PAGE + lane < n_keys, k=topk, n_live=n_tiles,
               search=n_keys > topk, rows=rows, width=PAGE)


def _dsa_dec_select(scores, *, topk, n_keys):
    n_tiles, rows, _ = scores.shape
    spec = pl.BlockSpec(scores.shape, lambda i: (0, 0, 0))
    return pl.pallas_call(
        functools.partial(_dsa_dec_select_kernel, topk=topk, n_tiles=n_tiles, rows=rows, n_keys=n_keys),
        grid=(1,),
        in_specs=[spec],
        out_specs=spec,
        out_shape=jax.ShapeDtypeStruct(scores.shape, F32),
        compiler_params=_params(("arbitrary",)),
        name="dsa_decode_select",
    )(scores)


def _dsa_dec_attn_kernel(pt_ref, q_ref, kvn_ref, bias_ref, *rest, n_pages, kw, group):
    kv_refs, o_ref = rest[:n_pages], rest[n_pages]
    nh = q_ref.shape[0]
    own = _head_of_lane((nh, kw), 1) == lax.shift_right_logical(_iota((nh, kw), 0), group.bit_length() - 1)
    r = _iota((HEAD_DIM, kw), 0)
    c = _iota((HEAD_DIM, kw), 1)
    spread = jnp.where((c & (HEAD_DIM - 1)) == r, 1.0, 0.0).astype(BF16)
    qb = jnp.where(own, _dot(q_ref[...], spread), 0.0)
    qblk = qb.astype(BF16)

    s_pages = [_dot_nt(qblk, kv_refs[j][:, 0:kw].astype(BF16)) + bias_ref[j] for j in range(n_pages)]
    kn = kvn_ref[:, 0:kw].astype(BF16).astype(F32)
    vn = kvn_ref[:, kw:2 * kw].astype(BF16).astype(F32)
    s_new = jnp.sum(qblk.astype(F32) * kn, axis=-1, keepdims=True) + bias_ref[n_pages][:, 0:1]

    m = s_new
    for s in s_pages:
        m = jnp.maximum(m, jnp.max(s, axis=-1, keepdims=True))
    p_new = jnp.exp(s_new - m)
    l = p_new
    acc = p_new * vn
    for j in range(n_pages):
        pr = jnp.exp(s_pages[j] - m)
        l = l + jnp.sum(pr, axis=-1, keepdims=True)
        acc = acc + _dot(pr.astype(BF16), kv_refs[j][:, kw:2 * kw].astype(BF16))
    o = jnp.where(own, acc / l, 0.0).astype(BF16)
    o_ref[...] = lax.dot_general(o, spread, NT, preferred_element_type=F32).astype(BF16)


def _dsa_dec_attn(page_table, q, kvn, bias, kvc, *, group):
    n, nh, _ = q.shape
    kw = kvn.shape[-1] // 2
    n_pages = page_table.shape[1]
    samp = lambda b, pt: (b, 0, 0)
    grid_spec = pltpu.PrefetchScalarGridSpec(
        num_scalar_prefetch=1,
        grid=(n,),
        in_specs=[
            pl.BlockSpec((None, nh, HEAD_DIM), samp),
            pl.BlockSpec((None, 1, 2 * kw), samp),
            pl.BlockSpec((n_pages + 1, None, 1, PAGE), lambda b, pt: (0, b, 0, 0)),
        ] + _page_specs((None, PAGE, 2 * kw), n_pages),
        out_specs=pl.BlockSpec((None, nh, HEAD_DIM), samp),
    )
    return pl.pallas_call(
        functools.partial(_dsa_dec_attn_kernel, n_pages=n_pages, kw=kw, group=group),
        grid_spec=grid_spec,
        out_shape=jax.ShapeDtypeStruct((n, nh, HEAD_DIM), BF16),
        compiler_params=_params(("arbitrary",)),
        name="dsa_decode_attn",
    )(page_table, q, kvn, bias, *([kvc] * n_pages))


def _block_diag(w):
    nb, bw, _ = w.shape
    eye = jnp.eye(nb, dtype=w.dtype)
    return jnp.einsum("nde,nm->ndme", w, eye).reshape(nb * bw, nb * bw)


def _rope_tables(pos):
    half = ROT_DIM // 2
    inv = ROPE_THETA ** (-jnp.arange(half, dtype=F32) / half)
    ang = pos.astype(F32)[:, None] * inv[None, :]
    cos, sin = jnp.cos(ang), jnp.sin(ang)
    rest = jnp.zeros((pos.shape[0], HEAD_DIM - ROT_DIM), F32)
    z = jnp.zeros_like(sin)
    cos_t = jnp.concatenate([cos, cos, rest + 1.0], axis=-1)
    sn_t = jnp.concatenate([-sin, z, rest], axis=-1)
    sp_t = jnp.concatenate([z, sin, rest], axis=-1)
    twice = lambda a: jnp.concatenate([a, a], axis=-1)
    return twice(cos_t), twice(sn_t), twice(sp_t)


def _pad_cols(w, width):
    return jnp.pad(w, ((0, 0), (0, width - w.shape[1])))


def kernel(x_prompt, x_sample, cache_fox_kv, cache_fox_logf, state_lru_conv, state_lru_h, cache_dsa_kv,
           cache_dsa_idx_k, page_table, norm_mix, norm_ffn, norm_final, ab_w_in, ab_b_f, ab_conv_w, ab_conv_b,
           ab_gate_a_w, ab_gate_a_b, ab_gate_x_w, ab_gate_x_b, ab_lambda, ab_w_out, c_w_in, c_idx_norm_g,
           c_idx_norm_b, c_w_out, ffn_w1, ffn_w2):
    batch, seq, d_model = x_prompt.shape
    n_dec = x_sample.shape[0]
    assert x_sample.shape[1] == 1
    n_pages = page_table.shape[1]
    past_len = n_pages * PAGE
    fox_heads = ab_b_f.shape[1]
    assert fox_heads == SUBLANES
    fw = fox_heads * HEAD_DIM
    rw = ab_lambda.shape[1]
    kvw = cache_dsa_kv.shape[4] * HEAD_DIM
    idx_heads = c_w_in.shape[2] - (d_model + 2 * kvw + HEAD_DIM)
    idx_heads = idx_heads // (HEAD_DIM + 1)
    qiw = idx_heads * HEAD_DIM
    dsa_heads = d_model // HEAD_DIM
    group = dsa_heads // (kvw // HEAD_DIM)
    assert ab_w_in.shape[2] == 3 * fw + fox_heads + 2 * rw
    assert c_w_in.shape[2] == d_model + 2 * kvw + qiw + HEAD_DIM + idx_heads

    w = ab_w_in[0]
    ab_main = jnp.concatenate([w[:, 0:3 * fw], w[:, 3 * fw + fox_heads:]], axis=1).astype(BF16)
    w_f = w[:, 3 * fw:3 * fw + fox_heads]
    ab_wf = _pad_cols(w_f, LANES).astype(BF16)
    ab_wft = w_f.T.astype(BF16)
    bf_col = _pad_cols(ab_b_f, LANES)
    bf_row = ab_b_f.reshape(fox_heads, 1)
    ga = _block_diag(ab_gate_a_w[0]).astype(BF16)
    gx = _block_diag(ab_gate_x_w[0]).astype(BF16)
    lru_w = (ab_conv_w[0], ab_conv_b, ga, ab_gate_a_b, gx, ab_gate_x_b, ab_lambda)
    ab_wout = ab_w_out[0].astype(BF16)
    cw = c_w_in[0]
    c_wq = cw[:, 0:d_model].astype(BF16)
    c_wkv = cw[:, d_model:d_model + 2 * kvw].astype(BF16)
    c_wqi = cw[:, d_model + 2 * kvw:d_model + 2 * kvw + qiw].astype(BF16)
    c_wkw = _pad_cols(cw[:, d_model + 2 * kvw + qiw:], LANES).astype(BF16)
    ng = _pad_cols(c_idx_norm_g, LANES)
    nb = _pad_cols(c_idx_norm_b, LANES)
    c_wout = c_w_out[0].astype(BF16)
    w1 = ffn_w1.astype(BF16)
    w2 = ffn_w2.astype(BF16)
    g_mix0, g_mix1 = norm_mix[0:1], norm_mix[1:2]
    g_ffn0, g_ffn1 = norm_ffn[0:1], norm_ffn[1:2]
    g_fin = norm_final.reshape(1, d_model)
    wi_scale = float(qiw) ** -0.5

    n_p = batch * seq
    h0 = x_prompt.reshape(n_p, d_model)
    q, kv, kvb, xr, gate, logf, ccol, crow = _fox_in(h0, g_mix0, ab_main, ab_wf, ab_wft, bf_col, bf_row,
                                                     batch=batch, seq=seq, fw=fw, rw=rw)
    attn = _fox_attn(q.reshape(batch, seq, fw), kvb.reshape(batch, seq, 2 * fw),
                     ccol.reshape(batch, seq, fox_heads), crow, n_heads=fox_heads)
    xr3 = xr.reshape(batch, seq, rw)
    yg, h_t = _lru(xr3, gate.reshape(batch, seq, rw), *lru_w)
    h1 = _out_ffn([attn.reshape(n_p, fw), yg.reshape(n_p, rw)], h0, ab_wout, g_ffn0, w1[0], w2[0])

    cos_p, sn_p, sp_p = _rope_tables(jnp.arange(seq, dtype=I32))
    q2, kv2, kdup, vdup, qi, ki, kidup, wi = _dsa_in(h1, g_mix1, c_wq, c_wkv, c_wqi, c_wkw, cos_p, sn_p, sp_p,
                                                     ng, nb, batch=batch, seq=seq, wi_scale=wi_scale)
    o2 = _dsa_attn(qi.reshape(batch, seq, qiw), wi.reshape(batch, seq, LANES), q2,
                   kidup.reshape(batch, seq, LANES), kdup, vdup, topk=min(TOPK_MAX, seq // 4))
    y_prompt = _out_ffn([o2], h1, c_wout, g_ffn1, w1[1], w2[1], g_fin, pair_major=True, seq=seq)

    hs0 = x_sample.reshape(n_dec, d_model)
    qs, kvs, _, xrs, gates, logfs, _, _ = _fox_in(hs0, g_mix0, ab_main, ab_wf, ab_wft, bf_col, bf_row,
                                                  batch=1, seq=n_dec, fw=fw, rw=rw)
    kvc = cache_fox_kv[0].reshape(-1, PAGE, 2 * fw)
    lfc = jnp.swapaxes(cache_fox_logf[0], 1, 2)
    attn_s = _fox_dec(page_table, qs.reshape(n_dec, 1, fw), kvs.reshape(n_dec, 1, 2 * fw),
                      logfs.reshape(n_dec, fox_heads, 1), kvc, lfc)
    conv_prev = jnp.swapaxes(state_lru_conv[0], 0, 1)
    ygs, h_ts = _lru_step(xrs, gates, conv_prev, state_lru_h[0], *lru_w)
    hs1 = _out_ffn([attn_s.reshape(n_dec, fw), ygs], hs0, ab_wout, g_ffn0, w1[0], w2[0])

    pos_s = jnp.full((n_dec,), past_len, I32)
    cos_s, sn_s, sp_s = _rope_tables(pos_s)
    q2s, kv2s, _, _, qis, kis, _, wis = _dsa_in(hs1, g_mix1, c_wq, c_wkv, c_wqi, c_wkw, cos_s, sn_s, sp_s,
                                                ng, nb, batch=1, seq=n_dec, wi_scale=wi_scale)
    q2s = jnp.swapaxes(q2s[0], 0, 1).reshape(n_dec, dsa_heads, HEAD_DIM)
    wis_h = wis[:, HEAD_DIM:HEAD_DIM + idx_heads].reshape(n_dec, idx_heads, 1)
    scores = _dsa_dec_scores(page_table, qis.reshape(n_dec, idx_heads, HEAD_DIM), wis_h,
                             kis.reshape(n_dec, 1, HEAD_DIM), cache_dsa_idx_k[0])
    n_keys = past_len + 1
    bias = _dsa_dec_select(scores.reshape(n_pages + 1, n_dec, PAGE), topk=min(TOPK_MAX, n_keys // 4),
                           n_keys=n_keys)
    o2s = _dsa_dec_attn(page_table, q2s, kv2s.reshape(n_dec, 1, 2 * kvw),
                        bias.reshape(n_pages + 1, n_dec, 1, PAGE), cache_dsa_kv[0].reshape(-1, PAGE, 2 * kvw),
                        group=group)
    y_sample = _out_ffn([o2s.reshape(n_dec, d_model)], hs1, c_wout, g_ffn1, w1[1], w2[1], g_fin)

    kv_heads = kvw // HEAD_DIM
    return (
        y_prompt.reshape(batch, seq, d_model),
        y_sample.reshape(n_dec, 1, d_model),
        kv.reshape(1, batch, seq, 2, fox_heads, HEAD_DIM),
        logf.reshape(1, batch, seq, fox_heads),
        xr3[:, seq - (CONV_W - 1):, :][None],
        h_t.reshape(1, batch, rw),
        kv2.reshape(1, batch, seq, 2, kv_heads, HEAD_DIM),
        ki.reshape(1, batch, seq, HEAD_DIM),
        kvs.reshape(1, n_dec, 1, 2, fox_heads, HEAD_DIM),
        logfs.reshape(1, n_dec, 1, fox_heads),
        jnp.concatenate([state_lru_conv[0][:, 1:, :], xrs[:, None, :]], axis=1)[None],
        h_ts.reshape(1, n_dec, rw),
        kv2s.reshape(1, n_dec, 1, 2, kv_heads, HEAD_DIM),
        kis.reshape(1, n_dec, 1, HEAD_DIM),
    )
```

```python
import functools

import jax
import jax.numpy as jnp
import numpy as np
from jax import lax
from jax.experimental import pallas as pl
from jax.experimental.pallas import tpu as pltpu

F32 = jnp.float32
BF16 = jnp.bfloat16
I32 = jnp.int32

EPS = 1e-6
HEAD_DIM = 64
ROT_DIM = HEAD_DIM // 4
ROPE_THETA = 500000.0
LRU_C = 8.0
CONV_W = 4
TOPK_MAX = 256
PAGE = 128
LANES = 128
SUBLANES = 8
NEG = -1e30
VMEM_LIMIT = 56 * 1024 * 1024

NT = (((1,), (1,)), ((), ()))


def _dot(a, b):
    return jnp.dot(a, b, preferred_element_type=F32)


def _dot_nt(a, b):
    return lax.dot_general(a, b, NT, preferred_element_type=F32)


def _split3(x):
    hi = x.astype(BF16)
    r1 = x - hi.astype(F32)
    mid = r1.astype(BF16)
    lo = (r1 - mid.astype(F32)).astype(BF16)
    return hi, mid, lo


def _dot_exact_l(e, x):
    hi, mid, lo = _split3(x)
    return _dot(e, hi) + _dot(e, mid) + _dot(e, lo)


def _dot_exact_r(x, e):
    hi, mid, lo = _split3(x)
    return _dot(hi, e) + _dot(mid, e) + _dot(lo, e)


def _rms(x, g):
    return x * lax.rsqrt(jnp.mean(x * x, axis=-1, keepdims=True) + EPS) * g


def _sigmoid(x):
    return 1.0 / (1.0 + jnp.exp(-x))


def _log_sigmoid(x):
    return jnp.minimum(x, 0.0) - jnp.log1p(jnp.exp(-jnp.abs(x)))


def _softplus(x):
    return jnp.maximum(x, 0.0) + jnp.log1p(jnp.exp(-jnp.abs(x)))


def _gelu_tanh(x):
    c = np.float32(np.sqrt(2.0 / np.pi))
    return 0.5 * x * (1.0 + jnp.tanh(c * (x + 0.044715 * (x * x * x))))


def _iota(shape, dim):
    return lax.broadcasted_iota(I32, shape, dim)


def _head_of_lane(shape, dim):
    return lax.shift_right_logical(_iota(shape, dim), HEAD_DIM.bit_length() - 1)


def _params(sem):
    return pltpu.CompilerParams(dimension_semantics=sem, vmem_limit_bytes=VMEM_LIMIT)


def _row_tile(n, pref):
    t = min(n, pref)
    assert n % t == 0
    return t


CK_LANE = HEAD_DIM
CQ_LANE = HEAD_DIM + 3
N_PIECES = 3


def _fox_place(n_heads):
    pk = np.zeros((N_PIECES * LANES, n_heads * LANES), np.float32)
    pq = np.zeros((N_PIECES * LANES, n_heads * LANES), np.float32)
    for i in range(N_PIECES):
        for h in range(n_heads):
            pk[i * LANES + h, h * LANES + CK_LANE + i] = -1.0
            pq[i * LANES + h, h * LANES + CQ_LANE + i] = 1.0
    return jnp.asarray(pk, BF16), jnp.asarray(pq, BF16)


def _fox_in_kernel(*refs, tm, tiles_per_batch, fw, rw, prompt):
    if prompt:
        (h_ref, g_ref, w_ref, wf_ref, bfc_ref, pk_ref, pq_ref,
         qa_ref, ka_ref, kv_ref, vb_ref, xr_ref, gate_ref, logf_ref, carry_c) = refs
    else:
        h_ref, g_ref, w_ref, wf_ref, bfc_ref, q_ref, kv_ref, xr_ref, gate_ref, logf_ref = refs
    xn = _rms(h_ref[...], g_ref[...]).astype(BF16)
    q = _dot(xn, w_ref[:, 0:fw]) * (HEAD_DIM ** -0.5)
    kv = _dot(xn, w_ref[:, fw:3 * fw])
    kv_ref[...] = kv
    xr_ref[...] = _dot(xn, w_ref[:, 3 * fw:3 * fw + rw])
    gate_ref[...] = _dot(xn, w_ref[:, 3 * fw + rw:3 * fw + 2 * rw])
    logf = _log_sigmoid(_dot(xn, wf_ref[...]) + bfc_ref[...])
    logf_ref[...] = logf[:, 0:SUBLANES]
    if not prompt:
        q_ref[...] = q.astype(BF16)
        return

    vb_ref[...] = kv[:, fw:2 * fw].astype(BF16)

    @pl.when(pl.program_id(0) % tiles_per_batch == 0)
    def _():
        carry_c[...] = jnp.zeros_like(carry_c)

    incl = jnp.where(_iota((tm, tm), 1) <= _iota((tm, tm), 0), 1.0, 0.0).astype(BF16)
    c = _dot_exact_l(incl, logf) + carry_c[...]
    carry_c[...] = c[tm - 1:tm, :]
    pieces = jnp.concatenate(_split3(c), axis=-1)
    lane = _iota((tm, LANES), 1)
    low = lane < HEAD_DIM
    k_one = jnp.where(jnp.logical_and(lane >= CQ_LANE, lane < CQ_LANE + N_PIECES), 1.0, 0.0)
    q_one = jnp.where(jnp.logical_and(lane >= CK_LANE, lane < CK_LANE + N_PIECES), 1.0, 0.0)
    for x, one, place_ref, out_ref in ((q, q_one, pq_ref, qa_ref), (kv, k_one, pk_ref, ka_ref)):
        for j in range(fw // LANES):
            pair = x[:, j * LANES:(j + 1) * LANES]
            for sub, own in ((0, pair), (1, pltpu.roll(pair, HEAD_DIM, axis=1))):
                h = 2 * j + sub
                aug = _dot(pieces, place_ref[:, h * LANES:(h + 1) * LANES]) + one
                out_ref[h] = jnp.where(low, own, aug).astype(BF16)


def _fox_in(h, g, w_main, wf, bfc, *, batch, seq, fw, rw, prompt):
    n, d = h.shape
    tm = _row_tile(seq, 256)
    tpb = seq // tm
    n_heads = fw // HEAD_DIM
    kern = functools.partial(_fox_in_kernel, tm=tm, tiles_per_batch=tpb, fw=fw, rw=rw, prompt=prompt)
    const = lambda i: (0, 0)
    row = lambda i: (i, 0)
    hm = lambda i: (i // tpb, 0, i % tpb, 0)
    args = [h, g, w_main, wf, bfc]
    in_specs = [pl.BlockSpec((tm, d), row), pl.BlockSpec((1, d), const), pl.BlockSpec(w_main.shape, const),
                pl.BlockSpec(wf.shape, const), pl.BlockSpec(bfc.shape, const)]
    tail_specs = [pl.BlockSpec((tm, rw), row), pl.BlockSpec((tm, rw), row), pl.BlockSpec((tm, SUBLANES), row)]
    tail_shapes = [jax.ShapeDtypeStruct((n, rw), F32), jax.ShapeDtypeStruct((n, rw), F32),
                   jax.ShapeDtypeStruct((n, SUBLANES), F32)]
    kv_spec, kv_shape = pl.BlockSpec((tm, 2 * fw), row), jax.ShapeDtypeStruct((n, 2 * fw), F32)
    if prompt:
        pk, pq = _fox_place(n_heads)
        args += [pk, pq]
        in_specs += [pl.BlockSpec(pk.shape, const), pl.BlockSpec(pq.shape, const)]
        aug_spec = pl.BlockSpec((None, n_heads, tm, LANES), hm)
        aug_shape = jax.ShapeDtypeStruct((batch, n_heads, seq, LANES), BF16)
        out_specs = [aug_spec, aug_spec, kv_spec, pl.BlockSpec((tm, fw), row)] + tail_specs
        out_shape = [aug_shape, aug_shape, kv_shape, jax.ShapeDtypeStruct((n, fw), BF16)] + tail_shapes
        scratch = [pltpu.VMEM((1, LANES), F32)]
    else:
        out_specs = [pl.BlockSpec((tm, fw), row), kv_spec] + tail_specs
        out_shape = [jax.ShapeDtypeStruct((n, fw), BF16), kv_shape] + tail_shapes
        scratch = []
    return pl.pallas_call(
        kern,
        grid=(n // tm,),
        in_specs=in_specs,
        out_specs=out_specs,
        out_shape=out_shape,
        scratch_shapes=scratch,
        compiler_params=_params(("arbitrary",)),
        name="fox_in_proj",
    )(*args)


def _softmax_step(s, v, m_ref, l_ref, acc_ref, idx):
    m_old = m_ref[idx]
    m_new = jnp.maximum(m_old, jnp.max(s, axis=-1, keepdims=True))
    alpha = jnp.exp(m_old - m_new)
    p = jnp.exp(s - jnp.concatenate([m_new] * (s.shape[-1] // LANES), axis=-1))
    psum = p[:, 0:LANES]
    for j in range(1, s.shape[-1] // LANES):
        psum = psum + p[:, j * LANES:(j + 1) * LANES]
    l_ref[idx] = alpha * l_ref[idx] + psum
    acc_ref[idx] = alpha * acc_ref[idx] + _dot(p.astype(BF16), v)
    m_ref[idx] = m_new


def _fox_attn_kernel(qa_ref, ka_ref, vb_ref, o_ref, m_ref, l_ref, acc_ref, *, tq, tk, n_heads):
    qi = pl.program_id(1)
    low = _iota((tq, LANES), 1) < HEAD_DIM
    n_full = (qi * tq) // tk
    qpos = qi * tq + _iota((tq, tk), 0)
    kloc = _iota((tq, tk), 1)
    m_ref[...] = jnp.full_like(m_ref, NEG)
    l_ref[...] = jnp.zeros_like(l_ref)
    acc_ref[...] = jnp.zeros_like(acc_ref)

    def tile(kt, masked):
        rows = pl.ds(pl.multiple_of(kt * tk, tk), tk)
        for h in range(n_heads):
            s = _dot_nt(qa_ref[h], ka_ref[h, rows, :])
            if masked:
                s = jnp.where(kt * tk + kloc <= qpos, s, NEG)
            _softmax_step(s, vb_ref[rows, (h // 2) * LANES:(h // 2 + 1) * LANES], m_ref, l_ref, acc_ref, h)

    def body(kt, carry):
        tile(kt, False)
        return carry

    lax.fori_loop(0, n_full, body, 0)
    tile(n_full, True)
    for hp in range(n_heads // 2):
        o0 = acc_ref[2 * hp] / jnp.sum(l_ref[2 * hp], axis=-1, keepdims=True)
        o1 = acc_ref[2 * hp + 1] / jnp.sum(l_ref[2 * hp + 1], axis=-1, keepdims=True)
        o_ref[:, hp * LANES:(hp + 1) * LANES] = jnp.where(low, o0, o1).astype(BF16)


def _fox_attn(qa, ka, vb):
    b, n_heads, t, _ = qa.shape
    fw = vb.shape[-1]
    tq = _row_tile(t, 128)
    tk = _row_tile(t, 256)
    kern = functools.partial(_fox_attn_kernel, tq=tq, tk=tk, n_heads=n_heads)
    return pl.pallas_call(
        kern,
        grid=(b, t // tq),
        in_specs=[
            pl.BlockSpec((None, n_heads, tq, LANES), lambda bi, qi: (bi, 0, qi, 0)),
            pl.BlockSpec((None, n_heads, t, LANES), lambda bi, qi: (bi, 0, 0, 0)),
            pl.BlockSpec((None, t, fw), lambda bi, qi: (bi, 0, 0)),
        ],
        out_specs=pl.BlockSpec((None, tq, fw), lambda bi, qi: (bi, qi, 0)),
        out_shape=jax.ShapeDtypeStruct((b, t, fw), BF16),
        scratch_shapes=[pltpu.VMEM((n_heads, tq, LANES), F32), pltpu.VMEM((n_heads, tq, LANES), F32),
                        pltpu.VMEM((n_heads, tq, LANES), F32)],
        compiler_params=_params(("arbitrary", "arbitrary")),
        name="fox_attn",
    )(qa, ka, vb)


def _lru_gates(xc, ga_ref, gab_ref, gx_ref, gxb_ref, sp):
    xcb = xc.astype(BF16)
    r = _sigmoid(_dot(xcb, ga_ref[...]) + gab_ref[...])
    i = _sigmoid(_dot(xcb, gx_ref[...]) + gxb_ref[...])
    log_a = -LRU_C * r * sp
    a = jnp.exp(log_a)
    u = jnp.sqrt(-jnp.tanh(log_a) * (a * a + 1.0)) * (i * xc)
    return a, u


def _lru_kernel(xr_ref, gate_ref, cw_ref, cb_ref, ga_ref, gab_ref, gx_ref, gxb_ref, lam_ref,
                yg_ref, ht_ref, xbuf, abuf, ubuf, hcar, *, tt):
    t = pl.program_id(1)
    rw = xr_ref.shape[-1]

    @pl.when(t == 0)
    def _():
        xbuf[0:SUBLANES, :] = jnp.zeros((SUBLANES, rw), F32)
        hcar[...] = jnp.zeros_like(hcar)

    x = xr_ref[...]
    xbuf[SUBLANES:SUBLANES + tt, :] = x
    x1 = xbuf[SUBLANES - 1:SUBLANES - 1 + tt, :]
    x2 = xbuf[SUBLANES - 2:SUBLANES - 2 + tt, :]
    x3 = xbuf[SUBLANES - 3:SUBLANES - 3 + tt, :]
    xc = cb_ref[...] + (cw_ref[0:1, :] * x3 + cw_ref[1:2, :] * x2 + cw_ref[2:3, :] * x1 + cw_ref[3:4, :] * x)
    xbuf[0:SUBLANES, :] = x[tt - SUBLANES:tt, :]

    a, u = _lru_gates(xc, ga_ref, gab_ref, gx_ref, gxb_ref, _softplus(-lam_ref[...]))
    abuf[...] = a
    ubuf[...] = u

    row = _iota((SUBLANES, rw), 0)

    def group(gi, hc):
        rows = pl.ds(pl.multiple_of(gi * SUBLANES, SUBLANES), SUBLANES)
        aa = abuf[rows, :]
        uu = ubuf[rows, :]
        for d in (1, 2, 4):
            keep = row >= d
            a_sh = jnp.where(keep, pltpu.roll(aa, d, axis=0), 1.0)
            u_sh = jnp.where(keep, pltpu.roll(uu, d, axis=0), 0.0)
            uu = aa * u_sh + uu
            aa = aa * a_sh
        hh = uu + aa * hc
        ubuf[rows, :] = hh
        return hh[SUBLANES - 1:SUBLANES, :]

    hc = lax.fori_loop(0, tt // SUBLANES, group, hcar[...])
    hcar[...] = hc
    ht_ref[...] = hc
    yg_ref[...] = (ubuf[...] * _gelu_tanh(gate_ref[...])).astype(BF16)


def _lru(xr, gate, cw, cb, ga, gab, gx, gxb, lam):
    b, t, rw = xr.shape
    tt = _row_tile(t, 256)
    const = lambda bi, ti: (0, 0)
    blk = pl.BlockSpec((None, tt, rw), lambda bi, ti: (bi, ti, 0))
    return pl.pallas_call(
        functools.partial(_lru_kernel, tt=tt),
        grid=(b, t // tt),
        in_specs=[blk, blk,
                  pl.BlockSpec(cw.shape, const), pl.BlockSpec(cb.shape, const),
                  pl.BlockSpec(ga.shape, const), pl.BlockSpec(gab.shape, const),
                  pl.BlockSpec(gx.shape, const), pl.BlockSpec(gxb.shape, const),
                  pl.BlockSpec(lam.shape, const)],
        out_specs=[blk, pl.BlockSpec((None, 1, rw), lambda bi, ti: (bi, 0, 0))],
        out_shape=[jax.ShapeDtypeStruct((b, t, rw), BF16), jax.ShapeDtypeStruct((b, 1, rw), F32)],
        scratch_shapes=[pltpu.VMEM((SUBLANES + tt, rw), F32), pltpu.VMEM((tt, rw), F32),
                        pltpu.VMEM((tt, rw), F32), pltpu.VMEM((1, rw), F32)],
        compiler_params=_params(("arbitrary", "arbitrary")),
        name="rglru",
    )(xr, gate, cw, cb, ga, gab, gx, gxb, lam)


def _lru_step_kernel(xr_ref, gate_ref, cp_ref, h0_ref, cw_ref, cb_ref, ga_ref, gab_ref, gx_ref, gxb_ref,
                     lam_ref, yg_ref, ht_ref):
    x = xr_ref[...]
    xc = cb_ref[...] + (cw_ref[0:1, :] * cp_ref[0] + cw_ref[1:2, :] * cp_ref[1] + cw_ref[2:3, :] * cp_ref[2]
                        + cw_ref[3:4, :] * x)
    a, u = _lru_gates(xc, ga_ref, gab_ref, gx_ref, gxb_ref, _softplus(-lam_ref[...]))
    h = a * h0_ref[...] + u
    ht_ref[...] = h
    yg_ref[...] = (h * _gelu_tanh(gate_ref[...])).astype(BF16)


def _lru_step(xr, gate, cp, h0, cw, cb, ga, gab, gx, gxb, lam):
    n, rw = xr.shape
    args = (xr, gate, cp, h0, cw, cb, ga, gab, gx, gxb, lam)
    return pl.pallas_call(
        _lru_step_kernel,
        grid=(1,),
        in_specs=[pl.BlockSpec(a.shape, lambda i, nd=a.ndim: (0,) * nd) for a in args],
        out_specs=[pl.BlockSpec((n, rw), lambda i: (0, 0)), pl.BlockSpec((n, rw), lambda i: (0, 0))],
        out_shape=[jax.ShapeDtypeStruct((n, rw), BF16), jax.ShapeDtypeStruct((n, rw), F32)],
        compiler_params=_params(("arbitrary",)),
        name="rglru_step",
    )(*args)


def _out_ffn_kernel(*refs, n_pieces, pair_major, final, ff_chunk):
    pieces = refs[:n_pieces]
    h_ref, wout_ref, g_ref, w1_ref, w2_ref = refs[n_pieces:n_pieces + 5]
    gf_ref = refs[n_pieces + 5] if final else None
    out_ref = refs[-1]

    mix = None
    off = 0
    for p in pieces:
        if pair_major:
            a = jnp.concatenate([p[j] for j in range(p.shape[0])], axis=-1)
        else:
            a = p[...]
        k = a.shape[-1]
        part = _dot(a, wout_ref[off:off + k, :])
        mix = part if mix is None else mix + part
        off += k
    h1 = h_ref[...] + mix
    xn = _rms(h1, g_ref[...]).astype(BF16)
    d_ff = w1_ref.shape[1]
    ffn = None
    for c in range(d_ff // ff_chunk):
        cols = slice(c * ff_chunk, (c + 1) * ff_chunk)
        hm = jnp.maximum(_dot(xn, w1_ref[:, cols]), 0.0)
        part = _dot((hm * hm).astype(BF16), w2_ref[cols, :])
        ffn = part if ffn is None else ffn + part
    y = h1 + ffn
    if final:
        y = _rms(y, gf_ref[...])
    out_ref[...] = y


def _out_ffn(pieces, h, wout, g, w1, w2, gf=None, *, pair_major=False, seq=None):
    n, d = h.shape
    tm = _row_tile(n if seq is None else seq, 256)
    tpb = None if seq is None else seq // tm
    const = lambda i: (0, 0)
    row = lambda i: (i, 0)
    in_specs = []
    for p in pieces:
        if pair_major:
            in_specs.append(pl.BlockSpec((None, p.shape[1], tm, p.shape[3]), lambda i: (i // tpb, 0, i % tpb, 0)))
        else:
            in_specs.append(pl.BlockSpec((tm, p.shape[1]), row))
    in_specs += [pl.BlockSpec((tm, d), row), pl.BlockSpec(wout.shape, const), pl.BlockSpec(g.shape, const),
                 pl.BlockSpec(w1.shape, const), pl.BlockSpec(w2.shape, const)]
    args = list(pieces) + [h, wout, g, w1, w2]
    if gf is not None:
        in_specs.append(pl.BlockSpec(gf.shape, const))
        args.append(gf)
    kern = functools.partial(_out_ffn_kernel, n_pieces=len(pieces), pair_major=pair_major,
                             final=gf is not None, ff_chunk=min(512, w1.shape[1]))
    return pl.pallas_call(
        kern,
        grid=(n // tm,),
        in_specs=in_specs,
        out_specs=pl.BlockSpec((tm, d), row),
        out_shape=jax.ShapeDtypeStruct((n, d), F32),
        compiler_params=_params(("arbitrary",)),
        name="out_proj_ffn",
    )(*args)


def _rope(x, cos, sn, sp):
    return x * cos + pltpu.roll(x, LANES - ROT_DIM // 2, axis=1) * sn + pltpu.roll(x, ROT_DIM // 2, axis=1) * sp


def _dup_halves(x, low):
    a = jnp.where(low, x, 0.0)
    b = x - a
    return a + pltpu.roll(a, HEAD_DIM, axis=1), b + pltpu.roll(b, HEAD_DIM, axis=1)


def _dsa_in_kernel(h_ref, g_ref, wq_ref, wkv_ref, wqi_ref, wkw_ref, wwt_ref, cos_ref, sn_ref, sp_ref, ng_ref,
                   nb_ref, q_ref, kv_ref, kdup_ref, vdup_ref, qi_ref, ki_ref, kidup_ref, wi_ref, wir_ref,
                   *, tm, wi_scale):
    xn = _rms(h_ref[...], g_ref[...]).astype(BF16)
    wir_ref[...] = _dot_nt(wwt_ref[...], xn) * wi_scale
    cos, sn, sp = cos_ref[...], sn_ref[...], sp_ref[...]
    low = _iota((tm, LANES), 1) < HEAD_DIM
    nq = wq_ref.shape[1] // LANES
    nkv = wkv_ref.shape[1] // (2 * LANES)
    nqi = wqi_ref.shape[1] // LANES

    for j in range(nq):
        x = _dot(xn, wq_ref[:, j * LANES:(j + 1) * LANES])
        q_ref[j] = (_rope(x, cos, sn, sp) * (HEAD_DIM ** -0.5)).astype(BF16)
    for j in range(nkv):
        k = _rope(_dot(xn, wkv_ref[:, j * LANES:(j + 1) * LANES]), cos, sn, sp)
        kv_ref[:, j * LANES:(j + 1) * LANES] = k
        ka, kb = _dup_halves(k, low)
        kdup_ref[2 * j] = ka.astype(BF16)
        kdup_ref[2 * j + 1] = kb.astype(BF16)
        v = _dot(xn, wkv_ref[:, (nkv + j) * LANES:(nkv + j + 1) * LANES])
        kv_ref[:, (nkv + j) * LANES:(nkv + j + 1) * LANES] = v
        va, vb = _dup_halves(v, low)
        vdup_ref[2 * j] = va.astype(BF16)
        vdup_ref[2 * j + 1] = vb.astype(BF16)
    for j in range(nqi):
        x = _dot(xn, wqi_ref[:, j * LANES:(j + 1) * LANES])
        qi_ref[:, j * LANES:(j + 1) * LANES] = _rope(x, cos, sn, sp).astype(BF16)

    kw = _dot(xn, wkw_ref[...])
    inv_n = 1.0 / HEAD_DIM
    mu = jnp.sum(jnp.where(low, kw, 0.0), axis=-1, keepdims=True) * inv_n
    cen = jnp.where(low, kw - mu, 0.0)
    var = jnp.sum(cen * cen, axis=-1, keepdims=True) * inv_n
    ki = _rope(cen * lax.rsqrt(var + EPS) * ng_ref[...] + nb_ref[...], cos, sn, sp)
    ki_ref[...] = ki[:, 0:HEAD_DIM]
    kidup_ref[...] = (ki + pltpu.roll(ki, HEAD_DIM, axis=1)).astype(BF16)
    wi_ref[...] = jnp.where(low, 0.0, kw * wi_scale)


def _dsa_in(h, g, wq, wkv, wqi, wkw, wwt, cos, sn, sp, ng, nb, *, batch, seq, wi_scale):
    n, d = h.shape
    tm = _row_tile(seq, 256)
    tpb = seq // tm
    nq, nkvp, qiw = wq.shape[1] // LANES, wkv.shape[1] // LANES, wqi.shape[1]
    const = lambda i: (0, 0)
    row = lambda i: (i, 0)
    pm = lambda i: (i // tpb, 0, i % tpb, 0)
    tab = pl.BlockSpec((tm, LANES), (lambda i: (i % tpb, 0)) if cos.shape[0] == seq else row)
    kern = functools.partial(_dsa_in_kernel, tm=tm, wi_scale=wi_scale)
    return pl.pallas_call(
        kern,
        grid=(n // tm,),
        in_specs=[pl.BlockSpec((tm, d), row), pl.BlockSpec(g.shape, const), pl.BlockSpec(wq.shape, const),
                  pl.BlockSpec(wkv.shape, const), pl.BlockSpec(wqi.shape, const), pl.BlockSpec(wkw.shape, const),
                  pl.BlockSpec(wwt.shape, const),
                  tab, tab, tab, pl.BlockSpec(ng.shape, const), pl.BlockSpec(nb.shape, const)],
        out_specs=[
            pl.BlockSpec((None, nq, tm, LANES), pm),
            pl.BlockSpec((tm, wkv.shape[1]), row),
            pl.BlockSpec((None, nkvp, tm, LANES), pm),
            pl.BlockSpec((None, nkvp, tm, LANES), pm),
            pl.BlockSpec((tm, qiw), row),
            pl.BlockSpec((tm, HEAD_DIM), row),
            pl.BlockSpec((tm, LANES), row),
            pl.BlockSpec((tm, LANES), row),
            pl.BlockSpec((None, wwt.shape[0], tm), lambda i: (i // tpb, 0, i % tpb)),
        ],
        out_shape=[
            jax.ShapeDtypeStruct((batch, nq, seq, LANES), BF16),
            jax.ShapeDtypeStruct((n, wkv.shape[1]), F32),
            jax.ShapeDtypeStruct((batch, nkvp, seq, LANES), BF16),
            jax.ShapeDtypeStruct((batch, nkvp, seq, LANES), BF16),
            jax.ShapeDtypeStruct((n, qiw), BF16),
            jax.ShapeDtypeStruct((n, HEAD_DIM), F32),
            jax.ShapeDtypeStruct((n, LANES), BF16),
            jax.ShapeDtypeStruct((n, LANES), F32),
            jax.ShapeDtypeStruct((batch, wwt.shape[0], seq), F32),
        ],
        compiler_params=_params(("arbitrary",)),
        name="dsa_in_proj",
    )(h, g, wq, wkv, wqi, wkw, wwt, cos, sn, sp, ng, nb)


def _ordered_bits(x):
    return x ^ ((x >> 31) & jnp.int32(0x7FFFFFFF))


def _topk_bias(sc_ref, bias_ref, valid_fn, *, k, n_live, search, depth, rows):
    int_min = jnp.int32(-2 ** 31)

    def fold(x):
        while x.shape[0] > SUBLANES:
            half = x.shape[0] // 2
            x = x[:half] + x[half:]
        return x

    def count(pred):
        tot = lax.fori_loop(0, n_live, lambda t, tot: tot + jnp.where(pred(sc_ref[t]), 1.0, 0.0),
                            jnp.zeros((depth, rows), F32))
        return jnp.sum(fold(tot), axis=0, keepdims=True)

    def to_float(key_u):
        return lax.bitcast_convert_type(_ordered_bits(key_u ^ int_min), F32)

    def bit_step(it, key_u):
        cand = key_u | lax.shift_left(jnp.int32(1), 31 - it)
        cnt = count(lambda s, c=to_float(cand): s >= c)
        return jnp.where(cnt >= k, cand, key_u)

    n_steps = jnp.where(search, 32, 0)
    tau = to_float(lax.fori_loop(0, n_steps, bit_step, jnp.zeros((1, rows), I32)))
    tau = jnp.where(count(lambda s: s > -jnp.inf) > k, tau, -jnp.inf)
    need = k - count(lambda s: s > tau)

    before = jnp.where(_iota((depth, depth), 1) < _iota((depth, depth), 0), 1.0, 0.0).astype(BF16)

    def write(t, seen):
        s = sc_ref[t]
        eq = s == tau
        eqf = jnp.where(eq, 1.0, 0.0)
        rank = _dot(before, eqf.astype(BF16)) + seen
        sel = jnp.logical_and(jnp.logical_or(s > tau, jnp.logical_and(eq, rank < need)), valid_fn(t))
        bias_ref[t] = jnp.where(sel, 0.0, NEG).T
        return seen + jnp.sum(fold(eqf), axis=0, keepdims=True)

    lax.fori_loop(0, n_live, write, jnp.zeros((1, rows), F32))


def _dsa_attn_kernel(qi_ref, wir_ref, q_ref, kid_ref, kdup_ref, vdup_ref, o_ref, sc_ref, bias_ref,
                     qim_ref, qs_ref, m_ref, l_ref, acc_ref, *, tq, tk, topk, n_idx_heads, n_kv, group):
    qt = pl.program_id(1)
    low = _iota((tq, LANES), 1) < HEAD_DIM
    n_full = (qt * tq) // tk
    causal_t = lambda t: t * tk + _iota((tk, tq), 0) <= qt * tq + _iota((tk, tq), 1)

    for j in range(n_idx_heads // 2):
        qp = qi_ref[:, j * LANES:(j + 1) * LANES]
        qim_ref[j, 0:tq, :] = jnp.where(low, qp, jnp.zeros_like(qp))
        qim_ref[j, tq:2 * tq, :] = jnp.where(low, jnp.zeros_like(qp), qp)

    def score_tile(kt, masked):
        kid = kid_ref[pl.ds(pl.multiple_of(kt * tk, tk), tk), :]
        sc = None
        for j in range(n_idx_heads // 2):
            d = jnp.maximum(_dot_nt(kid, qim_ref[j]), 0.0)
            term = d[:, 0:tq] * wir_ref[2 * j:2 * j + 1, :] + d[:, tq:2 * tq] * wir_ref[2 * j + 1:2 * j + 2, :]
            sc = term if sc is None else sc + term
        if masked:
            sc = jnp.where(causal_t(kt), sc, -jnp.inf)
        sc_ref[kt] = sc

    def score_body(kt, carry):
        score_tile(kt, False)
        return carry

    lax.fori_loop(0, n_full, score_body, 0)
    score_tile(n_full, True)

    _topk_bias(sc_ref, bias_ref, causal_t, k=topk, n_live=n_full + 1, search=(qt + 1) * tq > topk,
               depth=tk, rows=tq)

    pairs_per_kv = group // 2
    for n in range(n_kv):
        for j in range(pairs_per_kv):
            qp = q_ref[n * pairs_per_kv + j]
            qs_ref[n, (2 * j) * tq:(2 * j + 1) * tq, :] = jnp.where(low, qp, jnp.zeros_like(qp))
            qs_ref[n, (2 * j + 1) * tq:(2 * j + 2) * tq, :] = jnp.where(low, jnp.zeros_like(qp), qp)
    m_ref[...] = jnp.full_like(m_ref, NEG)
    l_ref[...] = jnp.zeros_like(l_ref)
    acc_ref[...] = jnp.zeros_like(acc_ref)

    def tile(kt, carry):
        rows = pl.ds(pl.multiple_of(kt * tk, tk), tk)
        b = bias_ref[kt]
        for n in range(n_kv):
            s = _dot_nt(qs_ref[n], kdup_ref[n, rows, :])
            s = (s.reshape(group, tq, tk) + b[None]).reshape(group * tq, tk)
            _softmax_step(s, vdup_ref[n, rows, :], m_ref, l_ref, acc_ref, n)
        return carry

    lax.fori_loop(0, n_full + 1, tile, 0)
    for n in range(n_kv):
        o = acc_ref[n] / jnp.sum(l_ref[n], axis=-1, keepdims=True)
        for j in range(pairs_per_kv):
            lo = o[(2 * j) * tq:(2 * j + 1) * tq]
            hi = o[(2 * j + 1) * tq:(2 * j + 2) * tq]
            o_ref[n * pairs_per_kv + j] = jnp.where(low, lo, hi).astype(BF16)


def _dsa_attn(qi, wi, q, kid, kdup, vdup, *, topk):
    b, t, qiw = qi.shape
    nq = q.shape[1]
    nkvp = kdup.shape[1]
    tq = _row_tile(t, 128)
    tk = _row_tile(t, 256)
    n_tiles = t // tk
    group = 2 * nq // nkvp
    n_idx_heads = wi.shape[1]
    kern = functools.partial(_dsa_attn_kernel, tq=tq, tk=tk, topk=topk,
                             n_idx_heads=n_idx_heads, n_kv=nkvp, group=group)
    return pl.pallas_call(
        kern,
        grid=(b, t // tq),
        in_specs=[
            pl.BlockSpec((None, tq, qiw), lambda bi, qt: (bi, qt, 0)),
            pl.BlockSpec((None, n_idx_heads, tq), lambda bi, qt: (bi, 0, qt)),
            pl.BlockSpec((None, nq, tq, LANES), lambda bi, qt: (bi, 0, qt, 0)),
            pl.BlockSpec((None, t, LANES), lambda bi, qt: (bi, 0, 0)),
            pl.BlockSpec((None, nkvp, t, LANES), lambda bi, qt: (bi, 0, 0, 0)),
            pl.BlockSpec((None, nkvp, t, LANES), lambda bi, qt: (bi, 0, 0, 0)),
        ],
        out_specs=pl.BlockSpec((None, nq, tq, LANES), lambda bi, qt: (bi, 0, qt, 0)),
        out_shape=jax.ShapeDtypeStruct((b, nq, t, LANES), BF16),
        scratch_shapes=[pltpu.VMEM((n_tiles, tk, tq), F32), pltpu.VMEM((n_tiles, tq, tk), F32),
                        pltpu.VMEM((n_idx_heads // 2, 2 * tq, LANES), BF16),
                        pltpu.VMEM((nkvp, group * tq, LANES), BF16), pltpu.VMEM((nkvp, group * tq, LANES), F32),
                        pltpu.VMEM((nkvp, group * tq, LANES), F32), pltpu.VMEM((nkvp, group * tq, LANES), F32)],
        compiler_params=_params(("arbitrary", "arbitrary")),
        name="dsa_attn",
    )(qi, wi, q, kid, kdup, vdup)


def _page_specs(block, n_pages):
    nd = len(block) - 1
    return [pl.BlockSpec(block, lambda b, pt, j=j: (pt[b, j],) + (0,) * nd) for j in range(n_pages)]


def _fox_dec_kernel(pt_ref, q_ref, kvn_ref, lfn_ref, *rest, n_pages, fw):
    kv_refs, lf_refs, o_ref = rest[:n_pages], rest[n_pages:2 * n_pages], rest[2 * n_pages]
    nh = SUBLANES
    diag = _head_of_lane((nh, fw), 1) == _iota((nh, fw), 0)
    qb = jnp.where(diag, jnp.broadcast_to(q_ref[...].astype(F32), (nh, fw)), 0.0)
    qblk = qb.astype(BF16)
    r = _iota((PAGE, PAGE), 0)
    c = _iota((PAGE, PAGE), 1)
    after = jnp.where(r > c, 1.0, 0.0).astype(BF16)

    car = lfn_ref[...]
    s_pages = [None] * n_pages
    for j in reversed(range(n_pages)):
        lf = lf_refs[j][...]
        k = kv_refs[j][:, 0:fw].astype(BF16)
        s_pages[j] = _dot_nt(qblk, k) + (_dot_exact_r(lf, after) + car)
        car = car + jnp.sum(lf, axis=-1, keepdims=True)
    s_new = jnp.sum(qb * kvn_ref[:, 0:fw], axis=-1, keepdims=True)

    m = s_new
    for s in s_pages:
        m = jnp.maximum(m, jnp.max(s, axis=-1, keepdims=True))
    p_new = jnp.exp(s_new - m)
    l = p_new
    acc = p_new * kvn_ref[:, fw:2 * fw]
    for j in range(n_pages):
        pr = jnp.exp(s_pages[j] - m)
        l = l + jnp.sum(pr, axis=-1, keepdims=True)
        acc = acc + _dot(pr.astype(BF16), kv_refs[j][:, fw:2 * fw].astype(BF16))
    o = jnp.where(diag, acc / l, 0.0)
    o_ref[...] = jnp.sum(o, axis=0, keepdims=True).astype(BF16)


def _fox_dec(page_table, q, kvn, lfn, kvc, lfc):
    n, _, fw = q.shape
    n_pages = page_table.shape[1]
    kern = functools.partial(_fox_dec_kernel, n_pages=n_pages, fw=fw)
    samp = lambda b, pt: (b, 0, 0)
    grid_spec = pltpu.PrefetchScalarGridSpec(
        num_scalar_prefetch=1,
        grid=(n,),
        in_specs=[
            pl.BlockSpec((None, 1, fw), samp),
            pl.BlockSpec((None, 1, 2 * fw), samp),
            pl.BlockSpec((None, SUBLANES, 1), samp),
        ] + _page_specs((None, PAGE, 2 * fw), n_pages) + _page_specs((None, SUBLANES, PAGE), n_pages),
        out_specs=pl.BlockSpec((None, 1, fw), samp),
    )
    return pl.pallas_call(
        kern,
        grid_spec=grid_spec,
        out_shape=jax.ShapeDtypeStruct((n, 1, fw), BF16),
        compiler_params=_params(("arbitrary",)),
        name="fox_decode_attn",
    )(page_table, q, kvn, lfn, *([kvc] * n_pages), *([lfc] * n_pages))


def _dsa_dec_score_kernel(pt_ref, qi_ref, wi_ref, kin_ref, *rest, n_pages):
    ki_refs, sc_ref = rest[:n_pages], rest[n_pages]
    qi = qi_ref[...]
    wi = wi_ref[...]
    for j in range(n_pages):
        d = _dot_nt(qi, ki_refs[j][...].astype(BF16))
        sc_ref[j] = jnp.sum(jnp.maximum(d, 0.0) * wi, axis=0, keepdims=True)
    kn = kin_ref[...].astype(BF16).astype(F32)
    d = jnp.sum(qi.astype(F32) * kn, axis=-1, keepdims=True)
    s_new = jnp.sum(jnp.maximum(d, 0.0) * wi, axis=0, keepdims=True)
    sc_ref[n_pages] = jnp.where(_iota((1, PAGE), 1) == 0, s_new, -jnp.inf)


def _dsa_dec_scores(page_table, qi, wi, kin, kic):
    n, nh, _ = qi.shape
    n_pages = page_table.shape[1]
    samp = lambda b, pt: (b, 0, 0)
    grid_spec = pltpu.PrefetchScalarGridSpec(
        num_scalar_prefetch=1,
        grid=(n,),
        in_specs=[
            pl.BlockSpec((None, nh, HEAD_DIM), samp),
            pl.BlockSpec((None, nh, 1), samp),
            pl.BlockSpec((None, 1, HEAD_DIM), samp),
        ] + _page_specs((None, PAGE, HEAD_DIM), n_pages),
        out_specs=pl.BlockSpec((n_pages + 1, None, 1, PAGE), lambda b, pt: (0, b, 0, 0)),
    )
    return pl.pallas_call(
        functools.partial(_dsa_dec_score_kernel, n_pages=n_pages),
        grid_spec=grid_spec,
        out_shape=jax.ShapeDtypeStruct((n_pages + 1, n, 1, PAGE), F32),
        compiler_params=_params(("arbitrary",)),
        name="dsa_decode_scores",
    )(page_table, qi, wi, kin, *([kic] * n_pages))


def _dsa_dec_select_kernel(sc_ref, bias_ref, sct_ref, *, topk, n_tiles, rows, n_keys):
    for t in range(n_tiles):
        sct_ref[t] = sc_ref[t].T
    key = _iota((PAGE, rows), 0)
    _topk_bias(sct_ref, bias_ref, lambda t: t * PAGE + key < n_keys, k=topk, n_live=n_tiles,
               search=n_keys > topk, depth=PAGE, rows=rows)


def _dsa_dec_select(scores, *, topk, n_keys):
    n_tiles, rows, _ = scores.shape
    spec = pl.BlockSpec(scores.shape, lambda i: (0, 0, 0))
    return pl.pallas_call(
        functools.partial(_dsa_dec_select_kernel, topk=topk, n_tiles=n_tiles, rows=rows, n_keys=n_keys),
        grid=(1,),
        in_specs=[spec],
        out_specs=spec,
        out_shape=jax.ShapeDtypeStruct(scores.shape, F32),
        scratch_shapes=[pltpu.VMEM((n_tiles, PAGE, rows), F32)],
        compiler_params=_params(("arbitrary",)),
        name="dsa_decode_select",
    )(scores)


def _dsa_dec_attn_kernel(pt_ref, q_ref, kvn_ref, bias_ref, *rest, n_pages, kw, group):
    kv_refs, o_ref = rest[:n_pages], rest[n_pages]
    nh = q_ref.shape[0]
    own = _head_of_lane((nh, kw), 1) == lax.shift_right_logical(_iota((nh, kw), 0), group.bit_length() - 1)
    r = _iota((HEAD_DIM, kw), 0)
    c = _iota((HEAD_DIM, kw), 1)
    spread = jnp.where((c & (HEAD_DIM - 1)) == r, 1.0, 0.0).astype(BF16)
    qb = jnp.where(own, _dot(q_ref[...], spread), 0.0)
    qblk = qb.astype(BF16)

    s_pages = [_dot_nt(qblk, kv_refs[j][:, 0:kw].astype(BF16)) + bias_ref[j] for j in range(n_pages)]
    kn = kvn_ref[:, 0:kw].astype(BF16).astype(F32)
    vn = kvn_ref[:, kw:2 * kw].astype(BF16).astype(F32)
    s_new = jnp.sum(qblk.astype(F32) * kn, axis=-1, keepdims=True) + bias_ref[n_pages][:, 0:1]

    m = s_new
    for s in s_pages:
        m = jnp.maximum(m, jnp.max(s, axis=-1, keepdims=True))
    p_new = jnp.exp(s_new - m)
    l = p_new
    acc = p_new * vn
    for j in range(n_pages):
        pr = jnp.exp(s_pages[j] - m)
        l = l + jnp.sum(pr, axis=-1, keepdims=True)
        acc = acc + _dot(pr.astype(BF16), kv_refs[j][:, kw:2 * kw].astype(BF16))
    o = jnp.where(own, acc / l, 0.0).astype(BF16)
    o_ref[...] = lax.dot_general(o, spread, NT, preferred_element_type=F32).astype(BF16)


def _dsa_dec_attn(page_table, q, kvn, bias, kvc, *, group):
    n, nh, _ = q.shape
    kw = kvn.shape[-1] // 2
    n_pages = page_table.shape[1]
    samp = lambda b, pt: (b, 0, 0)
    grid_spec = pltpu.PrefetchScalarGridSpec(
        num_scalar_prefetch=1,
        grid=(n,),
        in_specs=[
            pl.BlockSpec((None, nh, HEAD_DIM), samp),
            pl.BlockSpec((None, 1, 2 * kw), samp),
            pl.BlockSpec((n_pages + 1, None, 1, PAGE), lambda b, pt: (0, b, 0, 0)),
        ] + _page_specs((None, PAGE, 2 * kw), n_pages),
        out_specs=pl.BlockSpec((None, nh, HEAD_DIM), samp),
    )
    return pl.pallas_call(
        functools.partial(_dsa_dec_attn_kernel, n_pages=n_pages, kw=kw, group=group),
        grid_spec=grid_spec,
        out_shape=jax.ShapeDtypeStruct((n, nh, HEAD_DIM), BF16),
        compiler_params=_params(("arbitrary",)),
        name="dsa_decode_attn",
    )(page_table, q, kvn, bias, *([kvc] * n_pages))


def _block_diag(w):
    nb, bw, _ = w.shape
    eye = jnp.eye(nb, dtype=w.dtype)
    return jnp.einsum("nde,nm->ndme", w, eye).reshape(nb * bw, nb * bw)


def _rope_tables(pos):
    half = ROT_DIM // 2
    inv = ROPE_THETA ** (-jnp.arange(half, dtype=F32) / half)
    ang = pos.astype(F32)[:, None] * inv[None, :]
    cos, sin = jnp.cos(ang), jnp.sin(ang)
    rest = jnp.zeros((pos.shape[0], HEAD_DIM - ROT_DIM), F32)
    z = jnp.zeros_like(sin)
    cos_t = jnp.concatenate([cos, cos, rest + 1.0], axis=-1)
    sn_t = jnp.concatenate([-sin, z, rest], axis=-1)
    sp_t = jnp.concatenate([z, sin, rest], axis=-1)
    twice = lambda a: jnp.concatenate([a, a], axis=-1)
    return twice(cos_t), twice(sn_t), twice(sp_t)


def _pad_cols(w, width):
    return jnp.pad(w, ((0, 0), (0, width - w.shape[1])))


def kernel(x_prompt, x_sample, cache_fox_kv, cache_fox_logf, state_lru_conv, state_lru_h, cache_dsa_kv,
           cache_dsa_idx_k, page_table, norm_mix, norm_ffn, norm_final, ab_w_in, ab_b_f, ab_conv_w, ab_conv_b,
           ab_gate_a_w, ab_gate_a_b, ab_gate_x_w, ab_gate_x_b, ab_lambda, ab_w_out, c_w_in, c_idx_norm_g,
           c_idx_norm_b, c_w_out, ffn_w1, ffn_w2):
    batch, seq, d_model = x_prompt.shape
    n_dec = x_sample.shape[0]
    assert x_sample.shape[1] == 1
    n_pages = page_table.shape[1]
    past_len = n_pages * PAGE
    fox_heads = ab_b_f.shape[1]
    assert fox_heads == SUBLANES
    fw = fox_heads * HEAD_DIM
    rw = ab_lambda.shape[1]
    kvw = cache_dsa_kv.shape[4] * HEAD_DIM
    idx_heads = c_w_in.shape[2] - (d_model + 2 * kvw + HEAD_DIM)
    idx_heads = idx_heads // (HEAD_DIM + 1)
    qiw = idx_heads * HEAD_DIM
    dsa_heads = d_model // HEAD_DIM
    group = dsa_heads // (kvw // HEAD_DIM)
    assert ab_w_in.shape[2] == 3 * fw + fox_heads + 2 * rw
    assert c_w_in.shape[2] == d_model + 2 * kvw + qiw + HEAD_DIM + idx_heads

    w = ab_w_in[0]
    ab_main = jnp.concatenate([w[:, 0:3 * fw], w[:, 3 * fw + fox_heads:]], axis=1).astype(BF16)
    w_f = w[:, 3 * fw:3 * fw + fox_heads]
    ab_wf = _pad_cols(w_f, LANES).astype(BF16)
    bf_col = _pad_cols(ab_b_f, LANES)
    ga = _block_diag(ab_gate_a_w[0]).astype(BF16)
    gx = _block_diag(ab_gate_x_w[0]).astype(BF16)
    lru_w = (ab_conv_w[0], ab_conv_b, ga, ab_gate_a_b, gx, ab_gate_x_b, ab_lambda)
    ab_wout = ab_w_out[0].astype(BF16)
    cw = c_w_in[0]
    c_wq = cw[:, 0:d_model].astype(BF16)
    c_wkv = cw[:, d_model:d_model + 2 * kvw].astype(BF16)
    c_wqi = cw[:, d_model + 2 * kvw:d_model + 2 * kvw + qiw].astype(BF16)
    c_wkw = _pad_cols(cw[:, d_model + 2 * kvw + qiw:], LANES).astype(BF16)
    c_wwt = cw[:, d_model + 2 * kvw + qiw + HEAD_DIM:].T.astype(BF16)
    ng = _pad_cols(c_idx_norm_g, LANES)
    nb = _pad_cols(c_idx_norm_b, LANES)
    c_wout = c_w_out[0].astype(BF16)
    w1 = ffn_w1.astype(BF16)
    w2 = ffn_w2.astype(BF16)
    g_mix0, g_mix1 = norm_mix[0:1], norm_mix[1:2]
    g_ffn0, g_ffn1 = norm_ffn[0:1], norm_ffn[1:2]
    g_fin = norm_final.reshape(1, d_model)
    wi_scale = float(qiw) ** -0.5

    n_p = batch * seq
    h0 = x_prompt.reshape(n_p, d_model)
    qa, ka, kv, vb, xr, gate, logf = _fox_in(h0, g_mix0, ab_main, ab_wf, bf_col, batch=batch, seq=seq, fw=fw,
                                             rw=rw, prompt=True)
    attn = _fox_attn(qa, ka, vb.reshape(batch, seq, fw))
    xr3 = xr.reshape(batch, seq, rw)
    yg, h_t = _lru(xr3, gate.reshape(batch, seq, rw), *lru_w)
    h1 = _out_ffn([attn.reshape(n_p, fw), yg.reshape(n_p, rw)], h0, ab_wout, g_ffn0, w1[0], w2[0])

    cos_p, sn_p, sp_p = _rope_tables(jnp.arange(seq, dtype=I32))
    q2, kv2, kdup, vdup, qi, ki, kidup, _, wir = _dsa_in(h1, g_mix1, c_wq, c_wkv, c_wqi, c_wkw, c_wwt, cos_p, sn_p,
                                                         sp_p, ng, nb, batch=batch, seq=seq, wi_scale=wi_scale)
    o2 = _dsa_attn(qi.reshape(batch, seq, qiw), wir, q2, kidup.reshape(batch, seq, LANES), kdup, vdup,
                   topk=min(TOPK_MAX, seq // 4))
    y_prompt = _out_ffn([o2], h1, c_wout, g_ffn1, w1[1], w2[1], g_fin, pair_major=True, seq=seq)

    hs0 = x_sample.reshape(n_dec, d_model)
    qs, kvs, xrs, gates, logfs = _fox_in(hs0, g_mix0, ab_main, ab_wf, bf_col, batch=1, seq=n_dec, fw=fw, rw=rw,
                                         prompt=False)
    kvc = cache_fox_kv[0].reshape(-1, PAGE, 2 * fw)
    lfc = jnp.swapaxes(cache_fox_logf[0], 1, 2)
    attn_s = _fox_dec(page_table, qs.reshape(n_dec, 1, fw), kvs.reshape(n_dec, 1, 2 * fw),
                      logfs.reshape(n_dec, fox_heads, 1), kvc, lfc)
    conv_prev = jnp.swapaxes(state_lru_conv[0], 0, 1)
    ygs, h_ts = _lru_step(xrs, gates, conv_prev, state_lru_h[0], *lru_w)
    hs1 = _out_ffn([attn_s.reshape(n_dec, fw), ygs], hs0, ab_wout, g_ffn0, w1[0], w2[0])

    pos_s = jnp.full((n_dec,), past_len, I32)
    cos_s, sn_s, sp_s = _rope_tables(pos_s)
    q2s, kv2s, _, _, qis, kis, _, wis, _ = _dsa_in(hs1, g_mix1, c_wq, c_wkv, c_wqi, c_wkw, c_wwt, cos_s, sn_s,
                                                   sp_s, ng, nb, batch=1, seq=n_dec, wi_scale=wi_scale)
    q2s = jnp.swapaxes(q2s[0], 0, 1).reshape(n_dec, dsa_heads, HEAD_DIM)
    wis_h = wis[:, HEAD_DIM:HEAD_DIM + idx_heads].reshape(n_dec, idx_heads, 1)
    scores = _dsa_dec_scores(page_table, qis.reshape(n_dec, idx_heads, HEAD_DIM), wis_h,
                             kis.reshape(n_dec, 1, HEAD_DIM), cache_dsa_idx_k[0])
    n_keys = past_len + 1
    bias = _dsa_dec_select(scores.reshape(n_pages + 1, n_dec, PAGE), topk=min(TOPK_MAX, n_keys // 4),
                           n_keys=n_keys)
    o2s = _dsa_dec_attn(page_table, q2s, kv2s.reshape(n_dec, 1, 2 * kvw),
                        bias.reshape(n_pages + 1, n_dec, 1, PAGE), cache_dsa_kv[0].reshape(-1, PAGE, 2 * kvw),
                        group=group)
    y_sample = _out_ffn([o2s.reshape(n_dec, d_model)], hs1, c_wout, g_ffn1, w1[1], w2[1], g_fin)

    kv_heads = kvw // HEAD_DIM
    return (
        y_prompt.reshape(batch, seq, d_model),
        y_sample.reshape(n_dec, 1, d_model),
        kv.reshape(1, batch, seq, 2, fox_heads, HEAD_DIM),
        logf.reshape(1, batch, seq, fox_heads),
        xr3[:, seq - (CONV_W - 1):, :][None],
        h_t.reshape(1, batch, rw),
        kv2.reshape(1, batch, seq, 2, kv_heads, HEAD_DIM),
        ki.reshape(1, batch, seq, HEAD_DIM),
        kvs.reshape(1, n_dec, 1, 2, fox_heads, HEAD_DIM),
        logfs.reshape(1, n_dec, 1, fox_heads),
        jnp.concatenate([state_lru_conv[0][:, 1:, :], xrs[:, None, :]], axis=1)[None],
        h_ts.reshape(1, n_dec, rw),
        kv2s.reshape(1, n_dec, 1, 2, kv_heads, HEAD_DIM),
        kis.reshape(1, n_dec, 1, HEAD_DIM),
    )
```

```python
import functools

import jax
import jax.numpy as jnp
import numpy as np
from jax import lax
from jax.experimental import pallas as pl
from jax.experimental.pallas import tpu as pltpu

F32 = jnp.float32
BF16 = jnp.bfloat16
I32 = jnp.int32

EPS = 1e-6
HEAD_DIM = 64
ROT_DIM = HEAD_DIM // 4
ROPE_THETA = 500000.0
LRU_C = 8.0
CONV_W = 4
TOPK_MAX = 256
PAGE = 128
LANES = 128
SUBLANES = 8
NEG = -1e30
VMEM_LIMIT = 56 * 1024 * 1024

NT = (((1,), (1,)), ((), ()))


def _dot(a, b):
    return jnp.dot(a, b, preferred_element_type=F32)


def _dot_nt(a, b):
    return lax.dot_general(a, b, NT, preferred_element_type=F32)


def _split3(x):
    hi = x.astype(BF16)
    r1 = x - hi.astype(F32)
    mid = r1.astype(BF16)
    lo = (r1 - mid.astype(F32)).astype(BF16)
    return hi, mid, lo


def _dot_exact_l(e, x):
    hi, mid, lo = _split3(x)
    return _dot(e, hi) + _dot(e, mid) + _dot(e, lo)


def _dot_exact_r(x, e):
    hi, mid, lo = _split3(x)
    return _dot(hi, e) + _dot(mid, e) + _dot(lo, e)


def _rms(x, g):
    return x * lax.rsqrt(jnp.mean(x * x, axis=-1, keepdims=True) + EPS) * g


def _sigmoid(x):
    return 1.0 / (1.0 + jnp.exp(-x))


def _log_sigmoid(x):
    return jnp.minimum(x, 0.0) - jnp.log1p(jnp.exp(-jnp.abs(x)))


def _softplus(x):
    return jnp.maximum(x, 0.0) + jnp.log1p(jnp.exp(-jnp.abs(x)))


def _gelu_tanh(x):
    c = np.float32(np.sqrt(2.0 / np.pi))
    return 0.5 * x * (1.0 + jnp.tanh(c * (x + 0.044715 * (x * x * x))))


def _iota(shape, dim):
    return lax.broadcasted_iota(I32, shape, dim)


def _head_of_lane(shape, dim):
    return lax.shift_right_logical(_iota(shape, dim), HEAD_DIM.bit_length() - 1)


def _params(sem):
    return pltpu.CompilerParams(dimension_semantics=sem, vmem_limit_bytes=VMEM_LIMIT)


def _row_tile(n, pref):
    t = min(n, pref)
    assert n % t == 0
    return t


CK_LANE = HEAD_DIM
CQ_LANE = HEAD_DIM + 3
N_PIECES = 3


def _fox_place(n_heads):
    pk = np.zeros((N_PIECES * LANES, n_heads * LANES), np.float32)
    pq = np.zeros((N_PIECES * LANES, n_heads * LANES), np.float32)
    for i in range(N_PIECES):
        for h in range(n_heads):
            pk[i * LANES + h, h * LANES + CK_LANE + i] = -1.0
            pq[i * LANES + h, h * LANES + CQ_LANE + i] = 1.0
    return jnp.asarray(pk, BF16), jnp.asarray(pq, BF16)


def _fox_in_kernel(*refs, tm, tiles_per_batch, fw, rw, prompt):
    if prompt:
        (h_ref, g_ref, w_ref, wf_ref, bfc_ref, pk_ref, pq_ref,
         qa_ref, ka_ref, kv_ref, va_ref, xr_ref, gate_ref, logf_ref, carry_c) = refs
    else:
        h_ref, g_ref, w_ref, wf_ref, bfc_ref, q_ref, kv_ref, xr_ref, gate_ref, logf_ref = refs
    xn = _rms(h_ref[...], g_ref[...]).astype(BF16)
    q = _dot(xn, w_ref[:, 0:fw]) * (HEAD_DIM ** -0.5)
    kv = _dot(xn, w_ref[:, fw:3 * fw])
    kv_ref[...] = kv
    xr_ref[...] = _dot(xn, w_ref[:, 3 * fw:3 * fw + rw])
    gate_ref[...] = _dot(xn, w_ref[:, 3 * fw + rw:3 * fw + 2 * rw])
    logf = _log_sigmoid(_dot(xn, wf_ref[...]) + bfc_ref[...])
    logf_ref[...] = logf[:, 0:SUBLANES]
    if not prompt:
        q_ref[...] = q.astype(BF16)
        return

    lane = _iota((tm, LANES), 1)
    low = lane < HEAD_DIM
    for j in range(fw // LANES):
        va, vb = _with_ones(kv[:, fw + j * LANES:fw + (j + 1) * LANES], low)
        va_ref[2 * j] = va.astype(BF16)
        va_ref[2 * j + 1] = vb.astype(BF16)

    @pl.when(pl.program_id(0) % tiles_per_batch == 0)
    def _():
        carry_c[...] = jnp.zeros_like(carry_c)

    incl = jnp.where(_iota((tm, tm), 1) <= _iota((tm, tm), 0), 1.0, 0.0).astype(BF16)
    c = _dot_exact_l(incl, logf) + carry_c[...]
    carry_c[...] = c[tm - 1:tm, :]
    pieces = jnp.concatenate(_split3(c), axis=-1)
    k_one = jnp.where(jnp.logical_and(lane >= CQ_LANE, lane < CQ_LANE + N_PIECES), 1.0, 0.0)
    q_one = jnp.where(jnp.logical_and(lane >= CK_LANE, lane < CK_LANE + N_PIECES), 1.0, 0.0)
    for x, one, place_ref, out_ref in ((q, q_one, pq_ref, qa_ref), (kv, k_one, pk_ref, ka_ref)):
        for j in range(fw // LANES):
            pair = x[:, j * LANES:(j + 1) * LANES]
            for sub, own in ((0, pair), (1, pltpu.roll(pair, HEAD_DIM, axis=1))):
                h = 2 * j + sub
                aug = _dot(pieces, place_ref[:, h * LANES:(h + 1) * LANES]) + one
                out_ref[h] = jnp.where(low, own, aug).astype(BF16)


def _fox_in(h, g, w_main, wf, bfc, *, batch, seq, fw, rw, prompt):
    n, d = h.shape
    tm = _row_tile(seq, 256)
    tpb = seq // tm
    n_heads = fw // HEAD_DIM
    kern = functools.partial(_fox_in_kernel, tm=tm, tiles_per_batch=tpb, fw=fw, rw=rw, prompt=prompt)
    const = lambda i: (0, 0)
    row = lambda i: (i, 0)
    hm = lambda i: (i // tpb, 0, i % tpb, 0)
    args = [h, g, w_main, wf, bfc]
    in_specs = [pl.BlockSpec((tm, d), row), pl.BlockSpec((1, d), const), pl.BlockSpec(w_main.shape, const),
                pl.BlockSpec(wf.shape, const), pl.BlockSpec(bfc.shape, const)]
    tail_specs = [pl.BlockSpec((tm, rw), row), pl.BlockSpec((tm, rw), row), pl.BlockSpec((tm, SUBLANES), row)]
    tail_shapes = [jax.ShapeDtypeStruct((n, rw), F32), jax.ShapeDtypeStruct((n, rw), F32),
                   jax.ShapeDtypeStruct((n, SUBLANES), F32)]
    kv_spec, kv_shape = pl.BlockSpec((tm, 2 * fw), row), jax.ShapeDtypeStruct((n, 2 * fw), F32)
    if prompt:
        pk, pq = _fox_place(n_heads)
        args += [pk, pq]
        in_specs += [pl.BlockSpec(pk.shape, const), pl.BlockSpec(pq.shape, const)]
        aug_spec = pl.BlockSpec((None, n_heads, tm, LANES), hm)
        aug_shape = jax.ShapeDtypeStruct((batch, n_heads, seq, LANES), BF16)
        out_specs = [aug_spec, aug_spec, kv_spec, aug_spec] + tail_specs
        out_shape = [aug_shape, aug_shape, kv_shape, aug_shape] + tail_shapes
        scratch = [pltpu.VMEM((1, LANES), F32)]
    else:
        out_specs = [pl.BlockSpec((tm, fw), row), kv_spec] + tail_specs
        out_shape = [jax.ShapeDtypeStruct((n, fw), BF16), kv_shape] + tail_shapes
        scratch = []
    return pl.pallas_call(
        kern,
        grid=(n // tm,),
        in_specs=in_specs,
        out_specs=out_specs,
        out_shape=out_shape,
        scratch_shapes=scratch,
        compiler_params=_params(("arbitrary",)),
        name="fox_in_proj",
    )(*args)


def _with_ones(v, low):
    return jnp.where(low, v, 1.0), jnp.where(low, pltpu.roll(v, HEAD_DIM, axis=1), 1.0)


def _softmax_step(s, v1, m_ref, acc_ref, idx):
    m_old = m_ref[idx]
    m_new = jnp.maximum(m_old, jnp.max(s, axis=-1, keepdims=True))
    p = jnp.exp(s - jnp.concatenate([m_new] * (s.shape[-1] // LANES), axis=-1))
    acc_ref[idx] = jnp.exp(m_old - m_new) * acc_ref[idx] + _dot(p.astype(BF16), v1)
    m_ref[idx] = m_new


def _normalised(acc):
    return acc / pltpu.roll(acc, HEAD_DIM, axis=1)


def _fox_attn_kernel(qa_ref, ka_ref, va_ref, o_ref, m_ref, acc_ref, *, tq, tk, n_heads):
    qi = pl.program_id(1)
    low = _iota((tq, LANES), 1) < HEAD_DIM
    n_full = (qi * tq) // tk
    qpos = qi * tq + _iota((tq, tk), 0)
    kloc = _iota((tq, tk), 1)
    m_ref[...] = jnp.full_like(m_ref, NEG)
    acc_ref[...] = jnp.zeros_like(acc_ref)

    def tile(kt, masked):
        rows = pl.ds(pl.multiple_of(kt * tk, tk), tk)
        for h in range(n_heads):
            s = _dot_nt(qa_ref[h], ka_ref[h, rows, :])
            if masked:
                s = jnp.where(kt * tk + kloc <= qpos, s, NEG)
            _softmax_step(s, va_ref[h, rows, :], m_ref, acc_ref, h)

    def body(kt, carry):
        tile(kt, False)
        return carry

    lax.fori_loop(0, n_full, body, 0)
    tile(n_full, True)
    for hp in range(n_heads // 2):
        o0 = _normalised(acc_ref[2 * hp])
        o1 = _normalised(acc_ref[2 * hp + 1])
        o_ref[:, hp * LANES:(hp + 1) * LANES] = jnp.where(low, o0, pltpu.roll(o1, HEAD_DIM, axis=1)).astype(BF16)


def _fox_attn(qa, ka, va):
    b, n_heads, t, _ = qa.shape
    tq = _row_tile(t, 128)
    tk = _row_tile(t, 256)
    kern = functools.partial(_fox_attn_kernel, tq=tq, tk=tk, n_heads=n_heads)
    full = pl.BlockSpec((None, n_heads, t, LANES), lambda bi, qi: (bi, 0, 0, 0))
    return pl.pallas_call(
        kern,
        grid=(b, t // tq),
        in_specs=[pl.BlockSpec((None, n_heads, tq, LANES), lambda bi, qi: (bi, 0, qi, 0)), full, full],
        out_specs=pl.BlockSpec((None, tq, n_heads * HEAD_DIM), lambda bi, qi: (bi, qi, 0)),
        out_shape=jax.ShapeDtypeStruct((b, t, n_heads * HEAD_DIM), BF16),
        scratch_shapes=[pltpu.VMEM((n_heads, tq, LANES), F32), pltpu.VMEM((n_heads, tq, LANES), F32)],
        compiler_params=_params(("arbitrary", "arbitrary")),
        name="fox_attn",
    )(qa, ka, va)


def _lru_gates(xc, ga_ref, gab_ref, gx_ref, gxb_ref, sp):
    xcb = xc.astype(BF16)
    r = _sigmoid(_dot(xcb, ga_ref[...]) + gab_ref[...])
    i = _sigmoid(_dot(xcb, gx_ref[...]) + gxb_ref[...])
    log_a = -LRU_C * r * sp
    a = jnp.exp(log_a)
    u = jnp.sqrt(-jnp.tanh(log_a) * (a * a + 1.0)) * (i * xc)
    return a, u


def _lru_kernel(xr_ref, gate_ref, cw_ref, cb_ref, ga_ref, gab_ref, gx_ref, gxb_ref, lam_ref,
                yg_ref, ht_ref, xbuf, abuf, ubuf, hcar, *, tt):
    t = pl.program_id(1)
    rw = xr_ref.shape[-1]

    @pl.when(t == 0)
    def _():
        xbuf[0:SUBLANES, :] = jnp.zeros((SUBLANES, rw), F32)
        hcar[...] = jnp.zeros_like(hcar)

    x = xr_ref[...]
    xbuf[SUBLANES:SUBLANES + tt, :] = x
    x1 = xbuf[SUBLANES - 1:SUBLANES - 1 + tt, :]
    x2 = xbuf[SUBLANES - 2:SUBLANES - 2 + tt, :]
    x3 = xbuf[SUBLANES - 3:SUBLANES - 3 + tt, :]
    xc = cb_ref[...] + (cw_ref[0:1, :] * x3 + cw_ref[1:2, :] * x2 + cw_ref[2:3, :] * x1 + cw_ref[3:4, :] * x)
    xbuf[0:SUBLANES, :] = x[tt - SUBLANES:tt, :]

    a, u = _lru_gates(xc, ga_ref, gab_ref, gx_ref, gxb_ref, _softplus(-lam_ref[...]))
    abuf[...] = a
    ubuf[...] = u

    row = _iota((SUBLANES, rw), 0)

    def group(gi, hc):
        rows = pl.ds(pl.multiple_of(gi * SUBLANES, SUBLANES), SUBLANES)
        aa = abuf[rows, :]
        uu = ubuf[rows, :]
        for d in (1, 2, 4):
            keep = row >= d
            a_sh = jnp.where(keep, pltpu.roll(aa, d, axis=0), 1.0)
            u_sh = jnp.where(keep, pltpu.roll(uu, d, axis=0), 0.0)
            uu = aa * u_sh + uu
            aa = aa * a_sh
        hh = uu + aa * hc
        ubuf[rows, :] = hh
        return hh[SUBLANES - 1:SUBLANES, :]

    hc = lax.fori_loop(0, tt // SUBLANES, group, hcar[...])
    hcar[...] = hc
    ht_ref[...] = hc
    yg_ref[...] = (ubuf[...] * _gelu_tanh(gate_ref[...])).astype(BF16)


def _lru(xr, gate, cw, cb, ga, gab, gx, gxb, lam):
    b, t, rw = xr.shape
    tt = _row_tile(t, 256)
    const = lambda bi, ti: (0, 0)
    blk = pl.BlockSpec((None, tt, rw), lambda bi, ti: (bi, ti, 0))
    return pl.pallas_call(
        functools.partial(_lru_kernel, tt=tt),
        grid=(b, t // tt),
        in_specs=[blk, blk,
                  pl.BlockSpec(cw.shape, const), pl.BlockSpec(cb.shape, const),
                  pl.BlockSpec(ga.shape, const), pl.BlockSpec(gab.shape, const),
                  pl.BlockSpec(gx.shape, const), pl.BlockSpec(gxb.shape, const),
                  pl.BlockSpec(lam.shape, const)],
        out_specs=[blk, pl.BlockSpec((None, 1, rw), lambda bi, ti: (bi, 0, 0))],
        out_shape=[jax.ShapeDtypeStruct((b, t, rw), BF16), jax.ShapeDtypeStruct((b, 1, rw), F32)],
        scratch_shapes=[pltpu.VMEM((SUBLANES + tt, rw), F32), pltpu.VMEM((tt, rw), F32),
                        pltpu.VMEM((tt, rw), F32), pltpu.VMEM((1, rw), F32)],
        compiler_params=_params(("arbitrary", "arbitrary")),
        name="rglru",
    )(xr, gate, cw, cb, ga, gab, gx, gxb, lam)


def _lru_step_kernel(xr_ref, gate_ref, cp_ref, h0_ref, cw_ref, cb_ref, ga_ref, gab_ref, gx_ref, gxb_ref,
                     lam_ref, yg_ref, ht_ref):
    x = xr_ref[...]
    xc = cb_ref[...] + (cw_ref[0:1, :] * cp_ref[0] + cw_ref[1:2, :] * cp_ref[1] + cw_ref[2:3, :] * cp_ref[2]
                        + cw_ref[3:4, :] * x)
    a, u = _lru_gates(xc, ga_ref, gab_ref, gx_ref, gxb_ref, _softplus(-lam_ref[...]))
    h = a * h0_ref[...] + u
    ht_ref[...] = h
    yg_ref[...] = (h * _gelu_tanh(gate_ref[...])).astype(BF16)


def _lru_step(xr, gate, cp, h0, cw, cb, ga, gab, gx, gxb, lam):
    n, rw = xr.shape
    args = (xr, gate, cp, h0, cw, cb, ga, gab, gx, gxb, lam)
    return pl.pallas_call(
        _lru_step_kernel,
        grid=(1,),
        in_specs=[pl.BlockSpec(a.shape, lambda i, nd=a.ndim: (0,) * nd) for a in args],
        out_specs=[pl.BlockSpec((n, rw), lambda i: (0, 0)), pl.BlockSpec((n, rw), lambda i: (0, 0))],
        out_shape=[jax.ShapeDtypeStruct((n, rw), BF16), jax.ShapeDtypeStruct((n, rw), F32)],
        compiler_params=_params(("arbitrary",)),
        name="rglru_step",
    )(*args)


def _out_ffn_kernel(*refs, n_pieces, pair_major, final, ff_chunk):
    pieces = refs[:n_pieces]
    h_ref, wout_ref, g_ref, w1_ref, w2_ref = refs[n_pieces:n_pieces + 5]
    gf_ref = refs[n_pieces + 5] if final else None
    out_ref = refs[-1]

    mix = None
    off = 0
    for p in pieces:
        if pair_major:
            a = jnp.concatenate([p[j] for j in range(p.shape[0])], axis=-1)
        else:
            a = p[...]
        k = a.shape[-1]
        part = _dot(a, wout_ref[off:off + k, :])
        mix = part if mix is None else mix + part
        off += k
    h1 = h_ref[...] + mix
    xn = _rms(h1, g_ref[...]).astype(BF16)
    d_ff = w1_ref.shape[1]
    ffn = None
    for c in range(d_ff // ff_chunk):
        cols = slice(c * ff_chunk, (c + 1) * ff_chunk)
        hm = jnp.maximum(_dot(xn, w1_ref[:, cols]), 0.0)
        part = _dot((hm * hm).astype(BF16), w2_ref[cols, :])
        ffn = part if ffn is None else ffn + part
    y = h1 + ffn
    if final:
        y = _rms(y, gf_ref[...])
    out_ref[...] = y


def _out_ffn(pieces, h, wout, g, w1, w2, gf=None, *, pair_major=False, seq=None):
    n, d = h.shape
    tm = _row_tile(n if seq is None else seq, 256)
    tpb = None if seq is None else seq // tm
    const = lambda i: (0, 0)
    row = lambda i: (i, 0)
    in_specs = []
    for p in pieces:
        if pair_major:
            in_specs.append(pl.BlockSpec((None, p.shape[1], tm, p.shape[3]), lambda i: (i // tpb, 0, i % tpb, 0)))
        else:
            in_specs.append(pl.BlockSpec((tm, p.shape[1]), row))
    in_specs += [pl.BlockSpec((tm, d), row), pl.BlockSpec(wout.shape, const), pl.BlockSpec(g.shape, const),
                 pl.BlockSpec(w1.shape, const), pl.BlockSpec(w2.shape, const)]
    args = list(pieces) + [h, wout, g, w1, w2]
    if gf is not None:
        in_specs.append(pl.BlockSpec(gf.shape, const))
        args.append(gf)
    kern = functools.partial(_out_ffn_kernel, n_pieces=len(pieces), pair_major=pair_major,
                             final=gf is not None, ff_chunk=min(512, w1.shape[1]))
    return pl.pallas_call(
        kern,
        grid=(n // tm,),
        in_specs=in_specs,
        out_specs=pl.BlockSpec((tm, d), row),
        out_shape=jax.ShapeDtypeStruct((n, d), F32),
        compiler_params=_params(("arbitrary",)),
        name="out_proj_ffn",
    )(*args)


def _rope(x, cos, sn, sp):
    return x * cos + pltpu.roll(x, LANES - ROT_DIM // 2, axis=1) * sn + pltpu.roll(x, ROT_DIM // 2, axis=1) * sp


def _dup_halves(x, low):
    a = jnp.where(low, x, 0.0)
    b = x - a
    return a + pltpu.roll(a, HEAD_DIM, axis=1), b + pltpu.roll(b, HEAD_DIM, axis=1)


def _dsa_in_kernel(h_ref, g_ref, wq_ref, wkv_ref, wqi_ref, wkw_ref, wwt_ref, cos_ref, sn_ref, sp_ref, ng_ref,
                   nb_ref, q_ref, kv_ref, kdup_ref, vdup_ref, qi_ref, ki_ref, kidup_ref, wi_ref, wir_ref,
                   *, tm, wi_scale):
    xn = _rms(h_ref[...], g_ref[...]).astype(BF16)
    wir_ref[...] = _dot_nt(wwt_ref[...], xn) * wi_scale
    cos, sn, sp = cos_ref[...], sn_ref[...], sp_ref[...]
    low = _iota((tm, LANES), 1) < HEAD_DIM
    nq = wq_ref.shape[1] // LANES
    nkv = wkv_ref.shape[1] // (2 * LANES)
    nqi = wqi_ref.shape[1] // LANES

    for j in range(nq):
        x = _dot(xn, wq_ref[:, j * LANES:(j + 1) * LANES])
        q_ref[j] = (_rope(x, cos, sn, sp) * (HEAD_DIM ** -0.5)).astype(BF16)
    for j in range(nkv):
        k = _rope(_dot(xn, wkv_ref[:, j * LANES:(j + 1) * LANES]), cos, sn, sp)
        kv_ref[:, j * LANES:(j + 1) * LANES] = k
        ka, kb = _dup_halves(k, low)
        kdup_ref[2 * j] = ka.astype(BF16)
        kdup_ref[2 * j + 1] = kb.astype(BF16)
        v = _dot(xn, wkv_ref[:, (nkv + j) * LANES:(nkv + j + 1) * LANES])
        kv_ref[:, (nkv + j) * LANES:(nkv + j + 1) * LANES] = v
        va, vb = _with_ones(v, low)
        vdup_ref[2 * j] = va.astype(BF16)
        vdup_ref[2 * j + 1] = vb.astype(BF16)
    for j in range(nqi):
        x = _dot(xn, wqi_ref[:, j * LANES:(j + 1) * LANES])
        qi_ref[:, j * LANES:(j + 1) * LANES] = _rope(x, cos, sn, sp).astype(BF16)

    kw = _dot(xn, wkw_ref[...])
    inv_n = 1.0 / HEAD_DIM
    mu = jnp.sum(jnp.where(low, kw, 0.0), axis=-1, keepdims=True) * inv_n
    cen = jnp.where(low, kw - mu, 0.0)
    var = jnp.sum(cen * cen, axis=-1, keepdims=True) * inv_n
    ki = _rope(cen * lax.rsqrt(var + EPS) * ng_ref[...] + nb_ref[...], cos, sn, sp)
    ki_ref[...] = ki[:, 0:HEAD_DIM]
    kidup_ref[...] = (ki + pltpu.roll(ki, HEAD_DIM, axis=1)).astype(BF16)
    wi_ref[...] = jnp.where(low, 0.0, kw * wi_scale)


def _dsa_in(h, g, wq, wkv, wqi, wkw, wwt, cos, sn, sp, ng, nb, *, batch, seq, wi_scale):
    n, d = h.shape
    tm = _row_tile(seq, 256)
    tpb = seq // tm
    nq, nkvp, qiw = wq.shape[1] // LANES, wkv.shape[1] // LANES, wqi.shape[1]
    const = lambda i: (0, 0)
    row = lambda i: (i, 0)
    pm = lambda i: (i // tpb, 0, i % tpb, 0)
    tab = pl.BlockSpec((tm, LANES), (lambda i: (i % tpb, 0)) if cos.shape[0] == seq else row)
    kern = functools.partial(_dsa_in_kernel, tm=tm, wi_scale=wi_scale)
    return pl.pallas_call(
        kern,
        grid=(n // tm,),
        in_specs=[pl.BlockSpec((tm, d), row), pl.BlockSpec(g.shape, const), pl.BlockSpec(wq.shape, const),
                  pl.BlockSpec(wkv.shape, const), pl.BlockSpec(wqi.shape, const), pl.BlockSpec(wkw.shape, const),
                  pl.BlockSpec(wwt.shape, const),
                  tab, tab, tab, pl.BlockSpec(ng.shape, const), pl.BlockSpec(nb.shape, const)],
        out_specs=[
            pl.BlockSpec((None, nq, tm, LANES), pm),
            pl.BlockSpec((tm, wkv.shape[1]), row),
            pl.BlockSpec((None, nkvp, tm, LANES), pm),
            pl.BlockSpec((None, nkvp, tm, LANES), pm),
            pl.BlockSpec((tm, qiw), row),
            pl.BlockSpec((tm, HEAD_DIM), row),
            pl.BlockSpec((tm, LANES), row),
            pl.BlockSpec((tm, LANES), row),
            pl.BlockSpec((None, wwt.shape[0], tm), lambda i: (i // tpb, 0, i % tpb)),
        ],
        out_shape=[
            jax.ShapeDtypeStruct((batch, nq, seq, LANES), BF16),
            jax.ShapeDtypeStruct((n, wkv.shape[1]), F32),
            jax.ShapeDtypeStruct((batch, nkvp, seq, LANES), BF16),
            jax.ShapeDtypeStruct((batch, nkvp, seq, LANES), BF16),
            jax.ShapeDtypeStruct((n, qiw), BF16),
            jax.ShapeDtypeStruct((n, HEAD_DIM), F32),
            jax.ShapeDtypeStruct((n, LANES), BF16),
            jax.ShapeDtypeStruct((n, LANES), F32),
            jax.ShapeDtypeStruct((batch, wwt.shape[0], seq), F32),
        ],
        compiler_params=_params(("arbitrary",)),
        name="dsa_in_proj",
    )(h, g, wq, wkv, wqi, wkw, wwt, cos, sn, sp, ng, nb)


def _ordered_bits(x):
    return x ^ ((x >> 31) & jnp.int32(0x7FFFFFFF))


def _topk_bias(sc_ref, bias_ref, valid_fn, *, k, n_live, search, depth, rows):
    int_min = jnp.int32(-2 ** 31)

    def fold(x):
        while x.shape[0] > SUBLANES:
            half = x.shape[0] // 2
            x = x[:half] + x[half:]
        return x

    def count(pred):
        tot = lax.fori_loop(0, n_live, lambda t, tot: tot + jnp.where(pred(sc_ref[t]), 1.0, 0.0),
                            jnp.zeros((depth, rows), F32))
        return jnp.sum(fold(tot), axis=0, keepdims=True)

    def to_float(key_u):
        return lax.bitcast_convert_type(_ordered_bits(key_u ^ int_min), F32)

    def bit_step(it, key_u):
        cand = key_u | lax.shift_left(jnp.int32(1), 31 - it)
        cnt = count(lambda s, c=to_float(cand): s >= c)
        return jnp.where(cnt >= k, cand, key_u)

    n_steps = jnp.where(search, 32, 0)
    tau = to_float(lax.fori_loop(0, n_steps, bit_step, jnp.zeros((1, rows), I32)))
    tau = jnp.where(count(lambda s: s > -jnp.inf) > k, tau, -jnp.inf)
    need = k - count(lambda s: s > tau)

    before = jnp.where(_iota((depth, depth), 1) < _iota((depth, depth), 0), 1.0, 0.0).astype(BF16)

    def write(t, seen):
        s = sc_ref[t]
        eq = s == tau
        eqf = jnp.where(eq, 1.0, 0.0)
        rank = _dot(before, eqf.astype(BF16)) + seen
        sel = jnp.logical_and(jnp.logical_or(s > tau, jnp.logical_and(eq, rank < need)), valid_fn(t))
        bias_ref[t] = jnp.where(sel, 0.0, NEG).T
        return seen + jnp.sum(fold(eqf), axis=0, keepdims=True)

    lax.fori_loop(0, n_live, write, jnp.zeros((1, rows), F32))


def _dsa_attn_kernel(qi_ref, wir_ref, q_ref, kid_ref, kdup_ref, vone_ref, o_ref, sc_ref, bias_ref,
                     qim_ref, qs_ref, m_ref, acc_ref, *, tq, tk, topk, n_idx_heads, n_kv, group):
    qt = pl.program_id(1)
    low = _iota((tq, LANES), 1) < HEAD_DIM
    n_full = (qt * tq) // tk
    causal_t = lambda t: t * tk + _iota((tk, tq), 0) <= qt * tq + _iota((tk, tq), 1)

    for j in range(n_idx_heads // 2):
        qp = qi_ref[:, j * LANES:(j + 1) * LANES]
        qim_ref[j, 0:tq, :] = jnp.where(low, qp, jnp.zeros_like(qp))
        qim_ref[j, tq:2 * tq, :] = jnp.where(low, jnp.zeros_like(qp), qp)

    def score_tile(kt, masked):
        kid = kid_ref[pl.ds(pl.multiple_of(kt * tk, tk), tk), :]
        sc = None
        for j in range(n_idx_heads // 2):
            d = jnp.maximum(_dot_nt(kid, qim_ref[j]), 0.0)
            term = d[:, 0:tq] * wir_ref[2 * j:2 * j + 1, :] + d[:, tq:2 * tq] * wir_ref[2 * j + 1:2 * j + 2, :]
            sc = term if sc is None else sc + term
        if masked:
            sc = jnp.where(causal_t(kt), sc, -jnp.inf)
        sc_ref[kt] = sc

    def score_body(kt, carry):
        score_tile(kt, False)
        return carry

    lax.fori_loop(0, n_full, score_body, 0)
    score_tile(n_full, True)

    _topk_bias(sc_ref, bias_ref, causal_t, k=topk, n_live=n_full + 1, search=(qt + 1) * tq > topk,
               depth=tk, rows=tq)

    pairs_per_kv = group // 2
    for n in range(n_kv):
        for j in range(pairs_per_kv):
            qp = q_ref[n * pairs_per_kv + j]
            qs_ref[n, (2 * j) * tq:(2 * j + 1) * tq, :] = jnp.where(low, qp, jnp.zeros_like(qp))
            qs_ref[n, (2 * j + 1) * tq:(2 * j + 2) * tq, :] = jnp.where(low, jnp.zeros_like(qp), qp)
    m_ref[...] = jnp.full_like(m_ref, NEG)
    acc_ref[...] = jnp.zeros_like(acc_ref)

    def tile(kt, carry):
        rows = pl.ds(pl.multiple_of(kt * tk, tk), tk)
        b = bias_ref[kt]
        for n in range(n_kv):
            s = _dot_nt(qs_ref[n], kdup_ref[n, rows, :])
            s = (s.reshape(group, tq, tk) + b[None]).reshape(group * tq, tk)
            _softmax_step(s, vone_ref[n, rows, :], m_ref, acc_ref, n)
        return carry

    lax.fori_loop(0, n_full + 1, tile, 0)
    for n in range(n_kv):
        o = _normalised(acc_ref[n])
        for j in range(pairs_per_kv):
            lo = o[(2 * j) * tq:(2 * j + 1) * tq]
            hi = o[(2 * j + 1) * tq:(2 * j + 2) * tq]
            o_ref[n * pairs_per_kv + j] = jnp.where(low, lo, pltpu.roll(hi, HEAD_DIM, axis=1)).astype(BF16)


def _dsa_attn(qi, wi, q, kid, kdup, vdup, *, topk):
    b, t, qiw = qi.shape
    nq = q.shape[1]
    nkvp = kdup.shape[1]
    tq = _row_tile(t, 128)
    tk = _row_tile(t, 256)
    n_tiles = t // tk
    group = 2 * nq // nkvp
    n_idx_heads = wi.shape[1]
    kern = functools.partial(_dsa_attn_kernel, tq=tq, tk=tk, topk=topk,
                             n_idx_heads=n_idx_heads, n_kv=nkvp, group=group)
    return pl.pallas_call(
        kern,
        grid=(b, t // tq),
        in_specs=[
            pl.BlockSpec((None, tq, qiw), lambda bi, qt: (bi, qt, 0)),
            pl.BlockSpec((None, n_idx_heads, tq), lambda bi, qt: (bi, 0, qt)),
            pl.BlockSpec((None, nq, tq, LANES), lambda bi, qt: (bi, 0, qt, 0)),
            pl.BlockSpec((None, t, LANES), lambda bi, qt: (bi, 0, 0)),
            pl.BlockSpec((None, nkvp, t, LANES), lambda bi, qt: (bi, 0, 0, 0)),
            pl.BlockSpec((None, nkvp, t, LANES), lambda bi, qt: (bi, 0, 0, 0)),
        ],
        out_specs=pl.BlockSpec((None, nq, tq, LANES), lambda bi, qt: (bi, 0, qt, 0)),
        out_shape=jax.ShapeDtypeStruct((b, nq, t, LANES), BF16),
        scratch_shapes=[pltpu.VMEM((n_tiles, tk, tq), F32), pltpu.VMEM((n_tiles, tq, tk), F32),
                        pltpu.VMEM((n_idx_heads // 2, 2 * tq, LANES), BF16),
                        pltpu.VMEM((nkvp, group * tq, LANES), BF16), pltpu.VMEM((nkvp, group * tq, LANES), F32),
                        pltpu.VMEM((nkvp, group * tq, LANES), F32)],
        compiler_params=_params(("arbitrary", "arbitrary")),
        name="dsa_attn",
    )(qi, wi, q, kid, kdup, vdup)


def _page_specs(block, n_pages):
    nd = len(block) - 1
    return [pl.BlockSpec(block, lambda b, pt, j=j: (pt[b, j],) + (0,) * nd) for j in range(n_pages)]


def _fox_dec_kernel(pt_ref, q_ref, kvn_ref, lfn_ref, *rest, n_pages):
    kv_refs, lf_refs, o_ref = rest[:n_pages], rest[n_pages:2 * n_pages], rest[2 * n_pages]
    nh, hd = q_ref.shape
    half = PAGE // 2
    assert hd == half
    q8 = q_ref[...].astype(F32)
    after = jnp.where(_iota((PAGE, PAGE), 0) > _iota((PAGE, PAGE), 1), 1.0, 0.0).astype(BF16)
    pick = _iota((half, nh, hd), 2) == _iota((half, nh, hd), 0)

    m = jnp.sum(q8 * kvn_ref[0], axis=-1, keepdims=True)
    l = jnp.ones_like(m)
    acc = kvn_ref[1]
    car = lfn_ref[...]
    for j in reversed(range(n_pages)):
        lf = lf_refs[j][...]
        decay = _dot_exact_r(lf, after) + car
        car = car + jnp.sum(lf, axis=-1, keepdims=True)
        for part in reversed(range(2)):
            toks = pl.ds(part * half, half)
            spread = jnp.where(pick, decay[:, part * half:(part + 1) * half][None], 0.0)
            s = jnp.sum(kv_refs[j][toks, 0] * q8[None] + spread, axis=-1, keepdims=True)
            m_new = jnp.maximum(m, jnp.max(s, axis=0))
            alpha = jnp.exp(m - m_new)
            p = jnp.exp(s - m_new[None])
            l = alpha * l + jnp.sum(p, axis=0)
            acc = alpha * acc + jnp.sum(p * kv_refs[j][toks, 1], axis=0)
            m = m_new
    o_ref[...] = (acc / l).astype(BF16)


def _fox_dec(page_table, q, kvn, lfn, kvc, lfc):
    n, nh, hd = q.shape
    n_pages = page_table.shape[1]
    samp3 = lambda b, pt: (b, 0, 0)
    grid_spec = pltpu.PrefetchScalarGridSpec(
        num_scalar_prefetch=1,
        grid=(n,),
        in_specs=[
            pl.BlockSpec((None, nh, hd), samp3),
            pl.BlockSpec((None, 2, nh, hd), lambda b, pt: (b, 0, 0, 0)),
            pl.BlockSpec((None, nh, 1), samp3),
        ] + [pl.BlockSpec((None, None, PAGE, 2, nh, hd), lambda b, pt, j=j: (0, pt[b, j], 0, 0, 0, 0))
             for j in range(n_pages)] + _page_specs((None, nh, PAGE), n_pages),
        out_specs=pl.BlockSpec((None, nh, hd), samp3),
    )
    return pl.pallas_call(
        functools.partial(_fox_dec_kernel, n_pages=n_pages),
        grid_spec=grid_spec,
        out_shape=jax.ShapeDtypeStruct((n, nh, hd), BF16),
        compiler_params=_params(("arbitrary",)),
        name="fox_decode_attn",
    )(page_table, q, kvn, lfn, *([kvc] * n_pages), *([lfc] * n_pages))


def _dsa_dec_score_kernel(pt_ref, qi_ref, wi_ref, kin_ref, *rest, n_pages):
    ki_refs, sc_ref = rest[:n_pages], rest[n_pages]
    qi = qi_ref[...]
    wi = wi_ref[...]
    for j in range(n_pages):
        d = _dot_nt(qi, ki_refs[j][...].astype(BF16))
        sc_ref[j] = jnp.sum(jnp.maximum(d, 0.0) * wi, axis=0, keepdims=True)
    kn = kin_ref[...].astype(BF16).astype(F32)
    d = jnp.sum(qi.astype(F32) * kn, axis=-1, keepdims=True)
    s_new = jnp.sum(jnp.maximum(d, 0.0) * wi, axis=0, keepdims=True)
    sc_ref[n_pages] = jnp.where(_iota((1, PAGE), 1) == 0, s_new, -jnp.inf)


def _dsa_dec_scores(page_table, qi, wi, kin, kic):
    n, nh, _ = qi.shape
    n_pages = page_table.shape[1]
    samp = lambda b, pt: (b, 0, 0)
    grid_spec = pltpu.PrefetchScalarGridSpec(
        num_scalar_prefetch=1,
        grid=(n,),
        in_specs=[
            pl.BlockSpec((None, nh, HEAD_DIM), samp),
            pl.BlockSpec((None, nh, 1), samp),
            pl.BlockSpec((None, 1, HEAD_DIM), samp),
        ] + _page_specs((None, PAGE, HEAD_DIM), n_pages),
        out_specs=pl.BlockSpec((n_pages + 1, None, 1, PAGE), lambda b, pt: (0, b, 0, 0)),
    )
    return pl.pallas_call(
        functools.partial(_dsa_dec_score_kernel, n_pages=n_pages),
        grid_spec=grid_spec,
        out_shape=jax.ShapeDtypeStruct((n_pages + 1, n, 1, PAGE), F32),
        compiler_params=_params(("arbitrary",)),
        name="dsa_decode_scores",
    )(page_table, qi, wi, kin, *([kic] * n_pages))


def _dsa_dec_select_kernel(sc_ref, bias_ref, sct_ref, *, topk, n_tiles, rows, n_keys):
    for t in range(n_tiles):
        sct_ref[t] = sc_ref[t].T
    key = _iota((PAGE, rows), 0)
    _topk_bias(sct_ref, bias_ref, lambda t: t * PAGE + key < n_keys, k=topk, n_live=n_tiles,
               search=n_keys > topk, depth=PAGE, rows=rows)


def _dsa_dec_select(scores, *, topk, n_keys):
    n_tiles, rows, _ = scores.shape
    spec = pl.BlockSpec(scores.shape, lambda i: (0, 0, 0))
    return pl.pallas_call(
        functools.partial(_dsa_dec_select_kernel, topk=topk, n_tiles=n_tiles, rows=rows, n_keys=n_keys),
        grid=(1,),
        in_specs=[spec],
        out_specs=spec,
        out_shape=jax.ShapeDtypeStruct(scores.shape, F32),
        scratch_shapes=[pltpu.VMEM((n_tiles, PAGE, rows), F32)],
        compiler_params=_params(("arbitrary",)),
        name="dsa_decode_select",
    )(scores)


def _dsa_dec_attn_kernel(pt_ref, q_ref, kvn_ref, bias_ref, *rest, n_pages, kw, group):
    kv_refs, o_ref = rest[:n_pages], rest[n_pages]
    nh = q_ref.shape[0]
    own = _head_of_lane((nh, kw), 1) == lax.shift_right_logical(_iota((nh, kw), 0), group.bit_length() - 1)
    r = _iota((HEAD_DIM, kw), 0)
    c = _iota((HEAD_DIM, kw), 1)
    spread = jnp.where((c & (HEAD_DIM - 1)) == r, 1.0, 0.0).astype(BF16)
    qb = jnp.where(own, _dot(q_ref[...], spread), 0.0)
    qblk = qb.astype(BF16)

    s_pages = [_dot_nt(qblk, kv_refs[j][:, 0:kw].astype(BF16)) + bias_ref[j] for j in range(n_pages)]
    kn = kvn_ref[:, 0:kw].astype(BF16).astype(F32)
    vn = kvn_ref[:, kw:2 * kw].astype(BF16).astype(F32)
    s_new = jnp.sum(qblk.astype(F32) * kn, axis=-1, keepdims=True) + bias_ref[n_pages][:, 0:1]

    m = s_new
    for s in s_pages:
        m = jnp.maximum(m, jnp.max(s, axis=-1, keepdims=True))
    p_new = jnp.exp(s_new - m)
    l = p_new
    acc = p_new * vn
    for j in range(n_pages):
        pr = jnp.exp(s_pages[j] - m)
        l = l + jnp.sum(pr, axis=-1, keepdims=True)
        acc = acc + _dot(pr.astype(BF16), kv_refs[j][:, kw:2 * kw].astype(BF16))
    o = jnp.where(own, acc / l, 0.0).astype(BF16)
    o_ref[...] = lax.dot_general(o, spread, NT, preferred_element_type=F32).astype(BF16)


def _dsa_dec_attn(page_table, q, kvn, bias, kvc, *, group):
    n, nh, _ = q.shape
    kw = kvn.shape[-1] // 2
    n_pages = page_table.shape[1]
    samp = lambda b, pt: (b, 0, 0)
    grid_spec = pltpu.PrefetchScalarGridSpec(
        num_scalar_prefetch=1,
        grid=(n,),
        in_specs=[
            pl.BlockSpec((None, nh, HEAD_DIM), samp),
            pl.BlockSpec((None, 1, 2 * kw), samp),
            pl.BlockSpec((n_pages + 1, None, 1, PAGE), lambda b, pt: (0, b, 0, 0)),
        ] + _page_specs((None, PAGE, 2 * kw), n_pages),
        out_specs=pl.BlockSpec((None, nh, HEAD_DIM), samp),
    )
    return pl.pallas_call(
        functools.partial(_dsa_dec_attn_kernel, n_pages=n_pages, kw=kw, group=group),
        grid_spec=grid_spec,
        out_shape=jax.ShapeDtypeStruct((n, nh, HEAD_DIM), BF16),
        compiler_params=_params(("arbitrary",)),
        name="dsa_decode_attn",
    )(page_table, q, kvn, bias, *([kvc] * n_pages))


def _block_diag(w):
    nb, bw, _ = w.shape
    eye = jnp.eye(nb, dtype=w.dtype)
    return jnp.einsum("nde,nm->ndme", w, eye).reshape(nb * bw, nb * bw)


def _rope_tables(pos):
    half = ROT_DIM // 2
    inv = ROPE_THETA ** (-jnp.arange(half, dtype=F32) / half)
    ang = pos.astype(F32)[:, None] * inv[None, :]
    cos, sin = jnp.cos(ang), jnp.sin(ang)
    rest = jnp.zeros((pos.shape[0], HEAD_DIM - ROT_DIM), F32)
    z = jnp.zeros_like(sin)
    cos_t = jnp.concatenate([cos, cos, rest + 1.0], axis=-1)
    sn_t = jnp.concatenate([-sin, z, rest], axis=-1)
    sp_t = jnp.concatenate([z, sin, rest], axis=-1)
    twice = lambda a: jnp.concatenate([a, a], axis=-1)
    return twice(cos_t), twice(sn_t), twice(sp_t)


def _pad_cols(w, width):
    return jnp.pad(w, ((0, 0), (0, width - w.shape[1])))


def kernel(x_prompt, x_sample, cache_fox_kv, cache_fox_logf, state_lru_conv, state_lru_h, cache_dsa_kv,
           cache_dsa_idx_k, page_table, norm_mix, norm_ffn, norm_final, ab_w_in, ab_b_f, ab_conv_w, ab_conv_b,
           ab_gate_a_w, ab_gate_a_b, ab_gate_x_w, ab_gate_x_b, ab_lambda, ab_w_out, c_w_in, c_idx_norm_g,
           c_idx_norm_b, c_w_out, ffn_w1, ffn_w2):
    batch, seq, d_model = x_prompt.shape
    n_dec = x_sample.shape[0]
    assert x_sample.shape[1] == 1
    n_pages = page_table.shape[1]
    past_len = n_pages * PAGE
    fox_heads = ab_b_f.shape[1]
    assert fox_heads == SUBLANES
    fw = fox_heads * HEAD_DIM
    rw = ab_lambda.shape[1]
    kvw = cache_dsa_kv.shape[4] * HEAD_DIM
    idx_heads = c_w_in.shape[2] - (d_model + 2 * kvw + HEAD_DIM)
    idx_heads = idx_heads // (HEAD_DIM + 1)
    qiw = idx_heads * HEAD_DIM
    dsa_heads = d_model // HEAD_DIM
    group = dsa_heads // (kvw // HEAD_DIM)
    assert ab_w_in.shape[2] == 3 * fw + fox_heads + 2 * rw
    assert c_w_in.shape[2] == d_model + 2 * kvw + qiw + HEAD_DIM + idx_heads

    w = ab_w_in[0]
    ab_main = jnp.concatenate([w[:, 0:3 * fw], w[:, 3 * fw + fox_heads:]], axis=1).astype(BF16)
    w_f = w[:, 3 * fw:3 * fw + fox_heads]
    ab_wf = _pad_cols(w_f, LANES).astype(BF16)
    bf_col = _pad_cols(ab_b_f, LANES)
    ga = _block_diag(ab_gate_a_w[0]).astype(BF16)
    gx = _block_diag(ab_gate_x_w[0]).astype(BF16)
    lru_w = (ab_conv_w[0], ab_conv_b, ga, ab_gate_a_b, gx, ab_gate_x_b, ab_lambda)
    ab_wout = ab_w_out[0].astype(BF16)
    cw = c_w_in[0]
    c_wq = cw[:, 0:d_model].astype(BF16)
    c_wkv = cw[:, d_model:d_model + 2 * kvw].astype(BF16)
    c_wqi = cw[:, d_model + 2 * kvw:d_model + 2 * kvw + qiw].astype(BF16)
    c_wkw = _pad_cols(cw[:, d_model + 2 * kvw + qiw:], LANES).astype(BF16)
    c_wwt = cw[:, d_model + 2 * kvw + qiw + HEAD_DIM:].T.astype(BF16)
    ng = _pad_cols(c_idx_norm_g, LANES)
    nb = _pad_cols(c_idx_norm_b, LANES)
    c_wout = c_w_out[0].astype(BF16)
    w1 = ffn_w1.astype(BF16)
    w2 = ffn_w2.astype(BF16)
    g_mix0, g_mix1 = norm_mix[0:1], norm_mix[1:2]
    g_ffn0, g_ffn1 = norm_ffn[0:1], norm_ffn[1:2]
    g_fin = norm_final.reshape(1, d_model)
    wi_scale = float(qiw) ** -0.5

    n_p = batch * seq
    h0 = x_prompt.reshape(n_p, d_model)
    qa, ka, kv, va, xr, gate, logf = _fox_in(h0, g_mix0, ab_main, ab_wf, bf_col, batch=batch, seq=seq, fw=fw,
                                             rw=rw, prompt=True)
    attn = _fox_attn(qa, ka, va)
    xr3 = xr.reshape(batch, seq, rw)
    yg, h_t = _lru(xr3, gate.reshape(batch, seq, rw), *lru_w)
    h1 = _out_ffn([attn.reshape(n_p, fw), yg.reshape(n_p, rw)], h0, ab_wout, g_ffn0, w1[0], w2[0])

    cos_p, sn_p, sp_p = _rope_tables(jnp.arange(seq, dtype=I32))
    q2, kv2, kdup, vdup, qi, ki, kidup, _, wir = _dsa_in(h1, g_mix1, c_wq, c_wkv, c_wqi, c_wkw, c_wwt, cos_p, sn_p,
                                                         sp_p, ng, nb, batch=batch, seq=seq, wi_scale=wi_scale)
    o2 = _dsa_attn(qi.reshape(batch, seq, qiw), wir, q2, kidup.reshape(batch, seq, LANES), kdup, vdup,
                   topk=min(TOPK_MAX, seq // 4))
    y_prompt = _out_ffn([o2], h1, c_wout, g_ffn1, w1[1], w2[1], g_fin, pair_major=True, seq=seq)

    hs0 = x_sample.reshape(n_dec, d_model)
    qs, kvs, xrs, gates, logfs = _fox_in(hs0, g_mix0, ab_main, ab_wf, bf_col, batch=1, seq=n_dec, fw=fw, rw=rw,
                                         prompt=False)
    lfc = jnp.swapaxes(cache_fox_logf[0], 1, 2)
    attn_s = _fox_dec(page_table, qs.reshape(n_dec, fox_heads, HEAD_DIM),
                      kvs.reshape(n_dec, 2, fox_heads, HEAD_DIM), logfs.reshape(n_dec, fox_heads, 1),
                      cache_fox_kv, lfc)
    conv_prev = jnp.swapaxes(state_lru_conv[0], 0, 1)
    ygs, h_ts = _lru_step(xrs, gates, conv_prev, state_lru_h[0], *lru_w)
    hs1 = _out_ffn([attn_s.reshape(n_dec, fw), ygs], hs0, ab_wout, g_ffn0, w1[0], w2[0])

    pos_s = jnp.full((n_dec,), past_len, I32)
    cos_s, sn_s, sp_s = _rope_tables(pos_s)
    q2s, kv2s, _, _, qis, kis, _, wis, _ = _dsa_in(hs1, g_mix1, c_wq, c_wkv, c_wqi, c_wkw, c_wwt, cos_s, sn_s,
                                                   sp_s, ng, nb, batch=1, seq=n_dec, wi_scale=wi_scale)
    q2s = jnp.swapaxes(q2s[0], 0, 1).reshape(n_dec, dsa_heads, HEAD_DIM)
    wis_h = wis[:, HEAD_DIM:HEAD_DIM + idx_heads].reshape(n_dec, idx_heads, 1)
    scores = _dsa_dec_scores(page_table, qis.reshape(n_dec, idx_heads, HEAD_DIM), wis_h,
                             kis.reshape(n_dec, 1, HEAD_DIM), cache_dsa_idx_k[0])
    n_keys = past_len + 1
    bias = _dsa_dec_select(scores.reshape(n_pages + 1, n_dec, PAGE), topk=min(TOPK_MAX, n_keys // 4),
                           n_keys=n_keys)
    o2s = _dsa_dec_attn(page_table, q2s, kv2s.reshape(n_dec, 1, 2 * kvw),
                        bias.reshape(n_pages + 1, n_dec, 1, PAGE), cache_dsa_kv[0].reshape(-1, PAGE, 2 * kvw),
                        group=group)
    y_sample = _out_ffn([o2s.reshape(n_dec, d_model)], hs1, c_wout, g_ffn1, w1[1], w2[1], g_fin)

    kv_heads = kvw // HEAD_DIM
    return (
        y_prompt.reshape(batch, seq, d_model),
        y_sample.reshape(n_dec, 1, d_model),
        kv.reshape(1, batch, seq, 2, fox_heads, HEAD_DIM),
        logf.reshape(1, batch, seq, fox_heads),
        xr3[:, seq - (CONV_W - 1):, :][None],
        h_t.reshape(1, batch, rw),
        kv2.reshape(1, batch, seq, 2, kv_heads, HEAD_DIM),
        ki.reshape(1, batch, seq, HEAD_DIM),
        kvs.reshape(1, n_dec, 1, 2, fox_heads, HEAD_DIM),
        logfs.reshape(1, n_dec, 1, fox_heads),
        jnp.concatenate([state_lru_conv[0][:, 1:, :], xrs[:, None, :]], axis=1)[None],
        h_ts.reshape(1, n_dec, rw),
        kv2s.reshape(1, n_dec, 1, 2, kv_heads, HEAD_DIM),
        kis.reshape(1, n_dec, 1, HEAD_DIM),
    )
```
